```python
import jax
import jax.numpy as jnp
from jax import lax
import numpy as np


D_MODEL = 1024
BATCH = 16
SEQ = 2048
DEPTH = 1
DEC_BATCH = 4
DEC_SEQ = 4096
PAST_LEN = 128

N_META = 16
GLA_HEADS = 4
GLA_DK = 128
GLA_DV = 256
GLA_KEY_WIDTH = GLA_HEADS * GLA_DK
GLA_WIDTH = GLA_HEADS * GLA_DV
GLA_GATE_RANK = 16
GLA_TAU = 16.0
GLA_CHUNK = 64
MLA_HEADS = 8
MLA_D_NOPE = 128
MLA_D_ROPE = 64
MLA_D_V = 128
MLA_Q_RANK = 384
MLA_KV_RANK = 256
MLA_WIDTH = MLA_HEADS * MLA_D_V
Q_BLOCK = 128
ROPE_THETA = 10000.0
LN_EPS = 1e-5
RMS_EPS = 1e-6
DEEPNORM_ALPHA = (2 * DEPTH) ** 0.25
DEEPNORM_BETA = (8 * DEPTH) ** -0.25
IN_SPLITS = (GLA_KEY_WIDTH, GLA_KEY_WIDTH, GLA_WIDTH, GLA_WIDTH, GLA_GATE_RANK, GLA_GATE_RANK,
             MLA_Q_RANK, MLA_KV_RANK, MLA_D_ROPE, MLA_WIDTH, D_MODEL, D_MODEL)
IN_COLS = sum(IN_SPLITS)

kernel_name = 'hybrid_gla_mla_encoder'


def layer_norm(x, g, b):
    xf = x.astype(jnp.float32)
    mu = jnp.mean(xf, axis=-1, keepdims=True)
    var = jnp.mean(jnp.square(xf - mu), axis=-1, keepdims=True)
    return ((xf - mu) * lax.rsqrt(var + LN_EPS) * g.astype(jnp.float32) + b.astype(jnp.float32)).astype(x.dtype)


def rms_norm(x, g):
    xf = x.astype(jnp.float32)
    ms = jnp.mean(jnp.square(xf), axis=-1, keepdims=True)
    return (xf * lax.rsqrt(ms + RMS_EPS) * g.astype(jnp.float32)).astype(x.dtype)


def split_columns(a, sizes):
    parts, start = [], 0
    for size in sizes:
        parts.append(a[..., start:start + size])
        start += size
    return parts


def rope_tables(length):
    inv_freq = 1.0 / (ROPE_THETA ** (jnp.arange(0, MLA_D_ROPE, 2, dtype=jnp.float32) / MLA_D_ROPE))
    ang = jnp.arange(length, dtype=jnp.float32)[:, None] * inv_freq[None, :]
    return jnp.cos(ang), jnp.sin(ang)


def apply_rope(x, cos, sin):
    xf = x.astype(jnp.float32)
    x1, x2 = xf[..., :MLA_D_ROPE // 2], xf[..., MLA_D_ROPE // 2:]
    return jnp.concatenate([x1 * cos - x2 * sin, x1 * sin + x2 * cos], axis=-1).astype(x.dtype)


def gla_chunked(q, k, v, g, include_diag):
    B, T, H, _ = q.shape
    n = T // GLA_CHUNK

    def chunks(a):
        return a.astype(jnp.float32).reshape(B, n, GLA_CHUNK, H, a.shape[-1]).transpose(1, 0, 3, 2, 4)

    qc = chunks(q) * (GLA_DK ** -0.5)
    kc, vc = chunks(k), chunks(v)
    bc = jnp.cumsum(chunks(g), axis=3)
    b_last = bc[:, :, :, -1:, :]
    b_mid = bc[:, :, :, GLA_CHUNK // 2:GLA_CHUNK // 2 + 1, :]
    mask = jnp.tril(jnp.ones((GLA_CHUNK, GLA_CHUNK), dtype=bool), k=0 if include_diag else -1)
    a_intra = jnp.einsum('nbhid,nbhjd->nbhij', qc * jnp.exp(bc - b_mid), kc * jnp.exp(b_mid - bc))
    o_intra = jnp.einsum('nbhij,nbhjv->nbhiv', jnp.where(mask, a_intra, 0.0), vc)
    q_in = qc * jnp.exp(bc)
    k_dec = kc * jnp.exp(b_last - bc)
    decay = jnp.exp(b_last[:, :, :, 0, :])

    def step(state, xs):
        q_i, k_i, v_i, d_i = xs
        o_i = jnp.einsum('bhid,bhdv->bhiv', q_i, state)
        state = state * d_i[..., None] + jnp.einsum('bhjd,bhjv->bhdv', k_i, v_i)
        return state, o_i

    state0 = jnp.zeros((B, H, GLA_DK, GLA_DV), jnp.float32)
    _, o_inter = lax.scan(step, state0, (q_in, k_dec, vc, decay))
    o = o_intra + o_inter
    return o.transpose(1, 0, 3, 2, 4).reshape(B, T, H, GLA_DV)


def gla_bidirectional(q, k, v, g_fwd, g_bwd):
    pad = GLA_CHUNK - N_META
    widths = ((0, 0), (pad, 0), (0, 0), (0, 0))
    q, k, v, g_fwd, g_bwd = [jnp.pad(a, widths) for a in (q, k, v, g_fwd, g_bwd)]
    rev = lambda a: jnp.flip(a, axis=1)
    o_fwd = gla_chunked(q, k, v, g_fwd, True)
    o_bwd = rev(gla_chunked(rev(q), rev(k), rev(v), rev(g_bwd), False))
    return (o_fwd + o_bwd)[:, pad:].astype(v.dtype)


def mla_attention(q_nope, q_rope, k_nope, k_rope, v):
    B, T, H, _ = q_nope.shape
    n_blk = -(-T // Q_BLOCK)
    pad = n_blk * Q_BLOCK - T
    scale = (MLA_D_NOPE + MLA_D_ROPE) ** -0.5

    def blocks(a):
        a = jnp.pad(a, ((0, 0), (0, pad), (0, 0), (0, 0)))
        return a.reshape(B, n_blk, Q_BLOCK, H, a.shape[-1]).swapaxes(0, 1)

    def attend(qs):
        qn, qr = qs
        s = (jnp.einsum('bqhd,bkhd->bhqk', qn, k_nope, preferred_element_type=jnp.float32)
             + jnp.einsum('bqhr,bkr->bhqk', qr, k_rope, preferred_element_type=jnp.float32))
        p = jax.nn.softmax(s * scale, axis=-1).astype(v.dtype)
        return jnp.einsum('bhqk,bkhd->bqhd', p, v)

    o = lax.map(attend, (blocks(q_nope), blocks(q_rope)))
    return o.swapaxes(0, 1).reshape(B, n_blk * Q_BLOCK, H, MLA_D_V)[:, :T]


def encoder_layer(h, cos, sin, w_in, b_merge, w_gla_gate_f, b_gla_gate_f, w_gla_gate_b, b_gla_gate_b,
                  gla_norm_g, w_o_gla, q_a_norm_g, w_q_b, kv_a_norm_g, w_kv_b, w_o_mla, w_out,
                  post_ln_g, post_ln_b):
    B, L, _ = h.shape
    proj = h @ w_in
    gq, gk, gv, gr, glf, glb, cq, ckv, kr, mz, ma, mb = split_columns(proj, IN_SPLITS)

    g_f = jax.nn.log_sigmoid((glf @ w_gla_gate_f).astype(jnp.float32) + b_gla_gate_f) / GLA_TAU
    g_b = jax.nn.log_sigmoid((glb @ w_gla_gate_b).astype(jnp.float32) + b_gla_gate_b) / GLA_TAU
    heads = lambda a, d: a.reshape(B, L, -1, d)
    o_gla = gla_bidirectional(heads(gq, GLA_DK), heads(gk, GLA_DK), heads(gv, GLA_DV),
                              heads(g_f, GLA_DK), heads(g_b, GLA_DK))
    o_gla = rms_norm(o_gla, gla_norm_g).reshape(B, L, GLA_WIDTH) * jax.nn.silu(gr)
    branch_a = o_gla @ w_o_gla

    q = (rms_norm(cq, q_a_norm_g) @ w_q_b).reshape(B, L, MLA_HEADS, MLA_D_NOPE + MLA_D_ROPE)
    q_nope = q[..., :MLA_D_NOPE]
    q_rope = apply_rope(q[..., MLA_D_NOPE:], cos[:, None, :], sin[:, None, :])
    kv = (rms_norm(ckv, kv_a_norm_g) @ w_kv_b).reshape(B, L, MLA_HEADS, MLA_D_NOPE + MLA_D_V)
    k_nope, v = kv[..., :MLA_D_NOPE], kv[..., MLA_D_NOPE:]
    k_rope = apply_rope(kr, cos, sin)
    o_mla = mla_attention(q_nope, q_rope, k_nope, k_rope, v).reshape(B, L, MLA_WIDTH) * jax.nn.silu(mz)
    branch_b = o_mla @ w_o_mla

    mixed = jax.nn.sigmoid(ma + b_merge[:D_MODEL]) * branch_a + jax.nn.sigmoid(mb + b_merge[D_MODEL:]) * branch_b
    out = mixed @ w_out
    return layer_norm(DEEPNORM_ALPHA * h + out, post_ln_g, post_ln_b)


def encode(x, meta_tokens, emb_ln_g, emb_ln_b, layer_params):
    B, S, _ = x.shape
    L = N_META + S
    meta = jnp.broadcast_to(meta_tokens[None].astype(x.dtype), (B, N_META, D_MODEL))
    h = layer_norm(jnp.concatenate([meta, x], axis=1), emb_ln_g, emb_ln_b)
    cos, sin = rope_tables(L)
    for l in range(DEPTH):
        h = encoder_layer(h, cos, sin, *[p[l] for p in layer_params])
    return h[:, N_META:]


def setup_inputs(seed: int = 0) -> dict:
    key = jax.random.key(seed)
    ks = jax.random.split(key, 24)

    def nrm(k, shape, scale):
        return jax.random.normal(k, shape, jnp.float32) * scale

    def gain(k, shape):
        return 1.0 + nrm(k, shape, 0.02)

    x_prompt = nrm(ks[0], (BATCH, SEQ, D_MODEL), 1.0)
    x_sample = nrm(ks[1], (DEC_BATCH, DEC_SEQ, D_MODEL), 1.0)
    meta_tokens = nrm(ks[2], (N_META, D_MODEL), 1.0)
    emb_ln_g = gain(ks[3], (D_MODEL,))
    emb_ln_b = nrm(ks[4], (D_MODEL,), 0.02)
    v_lo = 2 * GLA_KEY_WIDTH
    w_in = nrm(ks[5], (DEPTH, D_MODEL, IN_COLS), D_MODEL ** -0.5)
    w_in = w_in.at[:, :, v_lo:v_lo + GLA_WIDTH].multiply(DEEPNORM_BETA)
    b_merge = nrm(ks[6], (DEPTH, 2 * D_MODEL), 0.02)
    w_gla_gate_f = nrm(ks[7], (DEPTH, GLA_GATE_RANK, GLA_KEY_WIDTH), GLA_GATE_RANK ** -0.5)
    b_gla_gate_f = nrm(ks[8], (DEPTH, GLA_KEY_WIDTH), 0.1)
    w_gla_gate_b = nrm(ks[9], (DEPTH, GLA_GATE_RANK, GLA_KEY_WIDTH), GLA_GATE_RANK ** -0.5)
    b_gla_gate_b = nrm(ks[10], (DEPTH, GLA_KEY_WIDTH), 0.1)
    gla_norm_g = gain(ks[11], (DEPTH, GLA_DV))
    w_o_gla = nrm(ks[12], (DEPTH, GLA_WIDTH, D_MODEL), GLA_WIDTH ** -0.5 * DEEPNORM_BETA)
    q_a_norm_g = gain(ks[13], (DEPTH, MLA_Q_RANK))
    w_q_b = nrm(ks[14], (DEPTH, MLA_Q_RANK, MLA_HEADS * (MLA_D_NOPE + MLA_D_ROPE)), MLA_Q_RANK ** -0.5)
    kv_a_norm_g = gain(ks[15], (DEPTH, MLA_KV_RANK))
    w_kv_b = nrm(ks[16], (DEPTH, MLA_KV_RANK, MLA_HEADS, MLA_D_NOPE + MLA_D_V), MLA_KV_RANK ** -0.5)
    w_kv_b = w_kv_b.at[..., MLA_D_NOPE:].multiply(DEEPNORM_BETA).reshape(
        DEPTH, MLA_KV_RANK, MLA_HEADS * (MLA_D_NOPE + MLA_D_V))
    w_o_mla = nrm(ks[17], (DEPTH, MLA_WIDTH, D_MODEL), MLA_WIDTH ** -0.5 * DEEPNORM_BETA)
    w_out = nrm(ks[18], (DEPTH, D_MODEL, D_MODEL), D_MODEL ** -0.5 * DEEPNORM_BETA)
    post_ln_g = gain(ks[19], (DEPTH, D_MODEL))
    post_ln_b = nrm(ks[20], (DEPTH, D_MODEL), 0.02)
    return {'x_prompt': x_prompt, 'x_sample': x_sample, 'meta_tokens': meta_tokens,
            'emb_ln_g': emb_ln_g, 'emb_ln_b': emb_ln_b, 'w_in': w_in, 'b_merge': b_merge,
            'w_gla_gate_f': w_gla_gate_f, 'b_gla_gate_f': b_gla_gate_f,
            'w_gla_gate_b': w_gla_gate_b, 'b_gla_gate_b': b_gla_gate_b, 'gla_norm_g': gla_norm_g,
            'w_o_gla': w_o_gla, 'q_a_norm_g': q_a_norm_g, 'w_q_b': w_q_b, 'kv_a_norm_g': kv_a_norm_g,
            'w_kv_b': w_kv_b, 'w_o_mla': w_o_mla, 'w_out': w_out,
            'post_ln_g': post_ln_g, 'post_ln_b': post_ln_b}


def reference(x_prompt, x_sample, meta_tokens, emb_ln_g, emb_ln_b, w_in, b_merge, w_gla_gate_f, b_gla_gate_f,
              w_gla_gate_b, b_gla_gate_b, gla_norm_g, w_o_gla, q_a_norm_g, w_q_b, kv_a_norm_g, w_kv_b,
              w_o_mla, w_out, post_ln_g, post_ln_b):
    layer_params = (w_in, b_merge, w_gla_gate_f, b_gla_gate_f, w_gla_gate_b, b_gla_gate_b, gla_norm_g,
                    w_o_gla, q_a_norm_g, w_q_b, kv_a_norm_g, w_kv_b, w_o_mla, w_out, post_ln_g, post_ln_b)
    y_prompt = encode(x_prompt, meta_tokens, emb_ln_g, emb_ln_b, layer_params)
    y_sample = encode(x_sample, meta_tokens, emb_ln_g, emb_ln_b, layer_params)
    return (y_prompt, y_sample)
```

```python
import functools

import jax
import jax.numpy as jnp
from jax import lax
from jax.experimental import pallas as pl
from jax.experimental.pallas import tpu as pltpu

D_MODEL = 1024
N_META = 16
GLA_HEADS = 4
GLA_DK = 128
GLA_DV = 256
GLA_KEY_WIDTH = GLA_HEADS * GLA_DK
GLA_WIDTH = GLA_HEADS * GLA_DV
GLA_GATE_RANK = 16
GLA_TAU = 16.0
GLA_CHUNK = 64
MLA_HEADS = 8
MLA_D_NOPE = 128
MLA_D_ROPE = 64
MLA_D_V = 128
MLA_Q_RANK = 384
MLA_KV_RANK = 256
MLA_WIDTH = MLA_HEADS * MLA_D_V
ROPE_THETA = 10000.0
LN_EPS = 1e-5
RMS_EPS = 1e-6
DEPTH = 1
DEEPNORM_ALPHA = (2 * DEPTH) ** 0.25
IN_SPLITS = (GLA_KEY_WIDTH, GLA_KEY_WIDTH, GLA_WIDTH, GLA_WIDTH, GLA_GATE_RANK, GLA_GATE_RANK,
             MLA_Q_RANK, MLA_KV_RANK, MLA_D_ROPE, MLA_WIDTH, D_MODEL, D_MODEL)

LANES = 128
MLA_QK_PAD = 2 * LANES
SMALL_COLS = LANES + MLA_Q_RANK + MLA_KV_RANK + LANES
VMEM_LIMIT = 56 * 1024 * 1024

F32 = jnp.float32
BF16 = jnp.bfloat16
NT_DIMS = (((1,), (1,)), ((), ()))
TN_DIMS = (((0,), (0,)), ((), ()))


def _dot(a, b):
    return jnp.dot(a, b, preferred_element_type=F32)


def _sigmoid(x):
    return 1.0 / (1.0 + jnp.exp(-x))


def _layer_norm(x, g, b):
    mu = jnp.mean(x, axis=-1, keepdims=True)
    xc = x - mu
    var = jnp.mean(xc * xc, axis=-1, keepdims=True)
    return xc * lax.rsqrt(var + LN_EPS) * g + b


def _rms_norm(x, g):
    ms = jnp.mean(x * x, axis=-1, keepdims=True)
    return x * lax.rsqrt(ms + RMS_EPS) * g


def _rope(blk, c, s1, s2):
    return blk * c + pltpu.roll(blk, 32, 1) * s1 + pltpu.roll(blk, 96, 1) * s2


def _inproj_kernel(x_ref, lng_ref, lnb_ref, wqk_ref, wv_ref, wr_ref, wz_ref, wa_ref, wb_ref, wsm_ref,
                   bma_ref, bmb_ref, wgate_ref, bgate_ref, qng_ref, wq_ref, kvng_ref, wkv_ref,
                   rc_ref, rs1_ref, rs2_ref,
                   gq_ref, gk_ref, gv_ref, gr_ref, gf_ref, gb_ref, q_ref, k_ref, v_ref, mz_ref,
                   sa_ref, sb_ref):
    h = _layer_norm(x_ref[...], lng_ref[...], lnb_ref[...])
    hb = h.astype(BF16)

    p = _dot(hb, wqk_ref[...])
    gq_ref[...] = (p[:, :GLA_KEY_WIDTH] * (GLA_DK ** -0.5)).astype(BF16)
    gk_ref[...] = p[:, GLA_KEY_WIDTH:].astype(BF16)
    gv_ref[...] = _dot(hb, wv_ref[...]).astype(BF16)
    r = _dot(hb, wr_ref[...])
    gr_ref[...] = (r * _sigmoid(r)).astype(BF16)
    z = _dot(hb, wz_ref[...])
    mz_ref[...] = (z * _sigmoid(z)).astype(BF16)
    sa_ref[...] = _sigmoid(_dot(hb, wa_ref[...]) + bma_ref[...]).astype(BF16)
    sb_ref[...] = _sigmoid(_dot(hb, wb_ref[...]) + bmb_ref[...]).astype(BF16)

    p = _dot(hb, wsm_ref[...])
    gx = _dot(p[:, :LANES].astype(BF16), wgate_ref[...]) + bgate_ref[...]
    ls = (jnp.minimum(gx, 0.0) - jnp.log1p(jnp.exp(-jnp.abs(gx)))) * (1.0 / GLA_TAU)
    gf_ref[...] = ls[:, :GLA_KEY_WIDTH]
    gb_ref[...] = ls[:, GLA_KEY_WIDTH:]

    rc, rs1, rs2 = rc_ref[...], rs1_ref[...], rs2_ref[...]
    scale = (MLA_D_NOPE + MLA_D_ROPE) ** -0.5
    cq = _rms_norm(p[:, LANES:LANES + MLA_Q_RANK], qng_ref[...]).astype(BF16)
    qf = _dot(cq, wq_ref[...])
    for hd in range(MLA_HEADS):
        lo = hd * MLA_QK_PAD
        q_ref[hd, :, :LANES] = (qf[:, lo:lo + LANES] * scale).astype(BF16)
        q_ref[hd, :, LANES:] = (_rope(qf[:, lo + LANES:lo + MLA_QK_PAD], rc, rs1, rs2) * scale).astype(BF16)

    kv_lo = LANES + MLA_Q_RANK
    ckv = _rms_norm(p[:, kv_lo:kv_lo + MLA_KV_RANK], kvng_ref[...]).astype(BF16)
    kvf = _dot(ckv, wkv_ref[...])
    kr = _rope(p[:, kv_lo + MLA_KV_RANK:], rc, rs1, rs2).astype(BF16)
    for hd in range(MLA_HEADS):
        k_ref[hd, :, :LANES] = kvf[:, hd * LANES:(hd + 1) * LANES].astype(BF16)
        k_ref[hd, :, LANES:] = kr
    v_ref[...] = kvf[:, MLA_HEADS * MLA_D_NOPE:].astype(BF16)


def _const_spec(shape):
    nd = len(shape)
    return pl.BlockSpec(shape, lambda i: (0,) * nd, pipeline_mode=pl.Buffered(1))


def _inproj(x2d, tm, seq_tiles, w, rope_tabs):
    n = x2d.shape[0]
    row = lambda cols: pl.BlockSpec((tm, cols), lambda i: (i, 0))
    tab = pl.BlockSpec((tm, LANES), lambda i: (i % seq_tiles, 0))
    consts = [w['emb_g'], w['emb_b'], w['w_qk'], w['w_v'], w['w_r'], w['w_z'], w['w_a'], w['w_b'], w['w_sm'],
              w['b_ma'], w['b_mb'], w['w_gate'], w['b_gate'], w['qn_g'], w['w_q'], w['kvn_g'], w['w_kv']]
    in_specs = [row(D_MODEL)] + [_const_spec(c.shape) for c in consts] + [tab, tab, tab]
    head_spec = pl.BlockSpec((MLA_HEADS, tm, MLA_QK_PAD), lambda i: (0, i, 0))
    out_shape = [
        jax.ShapeDtypeStruct((n, GLA_KEY_WIDTH), BF16), jax.ShapeDtypeStruct((n, GLA_KEY_WIDTH), BF16),
        jax.ShapeDtypeStruct((n, GLA_WIDTH), BF16), jax.ShapeDtypeStruct((n, GLA_WIDTH), BF16),
        jax.ShapeDtypeStruct((n, GLA_KEY_WIDTH), F32), jax.ShapeDtypeStruct((n, GLA_KEY_WIDTH), F32),
        jax.ShapeDtypeStruct((MLA_HEADS, n, MLA_QK_PAD), BF16),
        jax.ShapeDtypeStruct((MLA_HEADS, n, MLA_QK_PAD), BF16),
        jax.ShapeDtypeStruct((n, MLA_WIDTH), BF16), jax.ShapeDtypeStruct((n, MLA_WIDTH), BF16),
        jax.ShapeDtypeStruct((n, D_MODEL), BF16), jax.ShapeDtypeStruct((n, D_MODEL), BF16),
    ]
    out_specs = [row(GLA_KEY_WIDTH), row(GLA_KEY_WIDTH), row(GLA_WIDTH), row(GLA_WIDTH),
                 row(GLA_KEY_WIDTH), row(GLA_KEY_WIDTH), head_spec, head_spec,
                 row(MLA_WIDTH), row(MLA_WIDTH), row(D_MODEL), row(D_MODEL)]
    outs = pl.pallas_call(
        _inproj_kernel,
        grid=(n // tm,),
        in_specs=in_specs,
        out_specs=out_specs,
        out_shape=out_shape,
        compiler_params=pltpu.CompilerParams(dimension_semantics=("arbitrary",), vmem_limit_bytes=VMEM_LIMIT),
        name="inproj",
    )(x2d, *consts, *rope_tabs)
    names = ('gq', 'gk', 'gv', 'gr', 'gf', 'gb', 'q', 'k', 'v', 'mz', 'sa', 'sb')
    return dict(zip(names, outs))


def _chunk_cumsum(tri, g):
    g_hi = g.astype(BF16)
    g_lo = (g - g_hi.astype(F32)).astype(BF16)
    return _dot(tri, g_hi) + _dot(tri, g_lo)


def _decay_columns(b_last):
    d = jnp.transpose(jnp.broadcast_to(jnp.exp(b_last), (GLA_DK, GLA_DK)))
    return jnp.concatenate([d, d], axis=1)


def _gla_chunk(q, k, v, g, state, tri, mask, mid_row, last_row):
    bc = _chunk_cumsum(tri, g)
    b_mid = bc[mid_row:mid_row + 1]
    b_last = bc[last_row:last_row + 1]
    qa = (q * jnp.exp(bc - b_mid)).astype(BF16)
    ka = (k * jnp.exp(b_mid - bc)).astype(BF16)
    a = lax.dot_general(qa, ka, NT_DIMS, preferred_element_type=F32)
    a = jnp.where(mask, a, 0.0).astype(BF16)
    q_in = (q * jnp.exp(bc)).astype(BF16)
    o = _dot(a, v) + _dot(q_in, state.astype(BF16))
    k_dec = (k * jnp.exp(b_last - bc)).astype(BF16)
    new_state = state * _decay_columns(b_last) + lax.dot_general(k_dec, v, TN_DIMS, preferred_element_type=F32)
    return o, new_state


def _chunk_consts():
    c = GLA_CHUNK
    r = lax.broadcasted_iota(jnp.int32, (c, c), 0)
    s = lax.broadcasted_iota(jnp.int32, (c, c), 1)
    lower = r >= s
    upper = r <= s
    return (jnp.where(lower, 1.0, 0.0).astype(BF16), lower,
            jnp.where(upper, 1.0, 0.0).astype(BF16), r < s)


def _gla_kernel(q_ref, k_ref, v_ref, gf_ref, gb_ref, s0_ref, ng_ref, o_ref, acc_ref, sf_ref, sb_ref, *, n_chunks):
    c = GLA_CHUNK
    tri_f, mask_f, tri_b, mask_b = _chunk_consts()
    acc_ref[...] = jnp.zeros_like(acc_ref)
    sf_ref[...] = s0_ref[0]
    sb_ref[...] = jnp.zeros_like(sb_ref)

    def body(i, carry):
        rf = pl.multiple_of(i * c, c)
        rb = pl.multiple_of((n_chunks - 1 - i) * c, c)
        o_f, s_f = _gla_chunk(q_ref[pl.ds(rf, c), :].astype(F32), k_ref[pl.ds(rf, c), :].astype(F32),
                              v_ref[pl.ds(rf, c), :], gf_ref[pl.ds(rf, c), :], sf_ref[...],
                              tri_f, mask_f, c // 2, c - 1)
        sf_ref[...] = s_f
        acc_ref[pl.ds(rf, c), :] += o_f
        o_b, s_b = _gla_chunk(q_ref[pl.ds(rb, c), :].astype(F32), k_ref[pl.ds(rb, c), :].astype(F32),
                              v_ref[pl.ds(rb, c), :], gb_ref[pl.ds(rb, c), :], sb_ref[...],
                              tri_b, mask_b, c // 2 - 1, 0)
        sb_ref[...] = s_b
        acc_ref[pl.ds(rb, c), :] += o_b
        return carry

    lax.fori_loop(0, n_chunks, body, 0)

    blk = 4 * c

    def norm_body(i, carry):
        r0 = pl.multiple_of(i * blk, blk)
        o_ref[pl.ds(r0, blk), :] = _rms_norm(acc_ref[pl.ds(r0, blk), :], ng_ref[...]).astype(BF16)
        return carry

    lax.fori_loop(0, n_chunks * c // blk, norm_body, 0)


def _gla(p, s0, norm_g, batch, seq):
    n_chunks = seq // GLA_CHUNK
    kspec = pl.BlockSpec((seq, GLA_DK), lambda b, h: (b, h))
    vspec = pl.BlockSpec((seq, GLA_DV), lambda b, h: (b, h))
    return pl.pallas_call(
        functools.partial(_gla_kernel, n_chunks=n_chunks),
        grid=(batch, GLA_HEADS),
        in_specs=[kspec, kspec, vspec, kspec, kspec,
                  pl.BlockSpec((1, GLA_DK, GLA_DV), lambda b, h: (h, 0, 0)),
                  pl.BlockSpec((1, GLA_DV), lambda b, h: (0, 0))],
        out_specs=vspec,
        out_shape=jax.ShapeDtypeStruct((batch * seq, GLA_WIDTH), BF16),
        scratch_shapes=[pltpu.VMEM((seq, GLA_DV), F32), pltpu.VMEM((GLA_DK, GLA_DV), F32),
                        pltpu.VMEM((GLA_DK, GLA_DV), F32)],
        compiler_params=pltpu.CompilerParams(dimension_semantics=("arbitrary", "arbitrary"),
                                             vmem_limit_bytes=VMEM_LIMIT),
        name="gla",
    )(p['gq'], p['gk'], p['gv'], p['gf'], p['gb'], s0, norm_g)


def _gla_meta_state_kernel(k_ref, v_ref, g_ref, s_ref):
    tri_f, _, _, _ = _chunk_consts()
    bc = _chunk_cumsum(tri_f, g_ref[...])
    k_dec = (k_ref[...].astype(F32) * jnp.exp(bc[GLA_CHUNK - 1:GLA_CHUNK] - bc)).astype(BF16)
    s_ref[0] = lax.dot_general(k_dec, v_ref[...], TN_DIMS, preferred_element_type=F32)


def _gla_meta_state(k_pad, v_pad, g_pad):
    return pl.pallas_call(
        _gla_meta_state_kernel,
        grid=(GLA_HEADS,),
        in_specs=[pl.BlockSpec((GLA_CHUNK, GLA_DK), lambda h: (0, h)),
                  pl.BlockSpec((GLA_CHUNK, GLA_DV), lambda h: (0, h)),
                  pl.BlockSpec((GLA_CHUNK, GLA_DK), lambda h: (0, h))],
        out_specs=pl.BlockSpec((1, GLA_DK, GLA_DV), lambda h: (h, 0, 0)),
        out_shape=jax.ShapeDtypeStruct((GLA_HEADS, GLA_DK, GLA_DV), F32),
        name="gla_meta_state",
    )(k_pad, v_pad, g_pad)


def _attn_kernel(q_ref, k_ref, v_ref, km_ref, vm_ref, z_ref, o_ref, *, tk, n_kv):
    q = q_ref[0]
    s = lax.dot_general(q, km_ref[0], NT_DIMS, preferred_element_type=F32)
    m = jnp.max(s, axis=-1, keepdims=True)
    p = jnp.exp(s - m)
    l = jnp.sum(p, axis=-1, keepdims=True)
    acc = _dot(p.astype(BF16), vm_ref[...])

    def body(j, carry):
        m, l, acc = carry
        r0 = pl.multiple_of(j * tk, tk)
        s = lax.dot_general(q, k_ref[0, pl.ds(r0, tk), :], NT_DIMS, preferred_element_type=F32)
        m_new = jnp.maximum(m, jnp.max(s, axis=-1, keepdims=True))
        alpha = jnp.exp(m - m_new)
        p = jnp.exp(s - m_new)
        l = alpha * l + jnp.sum(p, axis=-1, keepdims=True)
        acc = alpha * acc + _dot(p.astype(BF16), v_ref[pl.ds(r0, tk), :])
        return m_new, l, acc

    m, l, acc = lax.fori_loop(0, n_kv, body, (m, l, acc))
    o_ref[...] = (acc / l * z_ref[...].astype(F32)).astype(BF16)


def _attention(p, meta, batch, seq, tq, tk):
    n_q = seq // tq
    return pl.pallas_call(
        functools.partial(_attn_kernel, tk=tk, n_kv=seq // tk),
        grid=(batch, MLA_HEADS, n_q),
        in_specs=[pl.BlockSpec((1, tq, MLA_QK_PAD), lambda b, h, i: (h, b * n_q + i, 0)),
                  pl.BlockSpec((1, seq, MLA_QK_PAD), lambda b, h, i: (h, b, 0)),
                  pl.BlockSpec((seq, MLA_D_V), lambda b, h, i: (b, h)),
                  pl.BlockSpec((1, N_META, MLA_QK_PAD), lambda b, h, i: (h, 0, 0)),
                  pl.BlockSpec((N_META, MLA_D_V), lambda b, h, i: (0, h)),
                  pl.BlockSpec((tq, MLA_D_V), lambda b, h, i: (b * n_q + i, h))],
        out_specs=pl.BlockSpec((tq, MLA_D_V), lambda b, h, i: (b * n_q + i, h)),
        out_shape=jax.ShapeDtypeStruct((batch * seq, MLA_WIDTH), BF16),
        compiler_params=pltpu.CompilerParams(dimension_semantics=("arbitrary", "arbitrary", "arbitrary"),
                                             vmem_limit_bytes=VMEM_LIMIT),
        name="mla_attention",
    )(p['q'], p['k'], p['v'], meta['k'], meta['v'], p['mz'])


def _out_kernel(x_ref, og_ref, gr_ref, om_ref, sa_ref, sb_ref, lng_ref, lnb_ref, woa_ref, wob_ref, wout_ref,
                png_ref, pnb_ref, y_ref):
    h = _layer_norm(x_ref[...], lng_ref[...], lnb_ref[...])
    a_in = (og_ref[...].astype(F32) * gr_ref[...].astype(F32)).astype(BF16)
    branch_a = _dot(a_in, woa_ref[...])
    branch_b = _dot(om_ref[...], wob_ref[...])
    mixed = sa_ref[...].astype(F32) * branch_a + sb_ref[...].astype(F32) * branch_b
    out = _dot(mixed.astype(BF16), wout_ref[...])
    y_ref[...] = _layer_norm(DEEPNORM_ALPHA * h + out, png_ref[...], pnb_ref[...])


def _out_stage(x2d, p, o_gla, o_mla, w, tm):
    n = x2d.shape[0]
    row = pl.BlockSpec((tm, D_MODEL), lambda i: (i, 0))
    consts = [w['emb_g'], w['emb_b'], w['w_o_gla'], w['w_o_mla'], w['w_out'], w['post_g'], w['post_b']]
    return pl.pallas_call(
        _out_kernel,
        grid=(n // tm,),
        in_specs=[row] * 6 + [_const_spec(c.shape) for c in consts],
        out_specs=row,
        out_shape=jax.ShapeDtypeStruct((n, D_MODEL), F32),
        compiler_params=pltpu.CompilerParams(dimension_semantics=("arbitrary",), vmem_limit_bytes=VMEM_LIMIT),
        name="merge_out",
    )(x2d, o_gla, p['gr'], o_mla, p['sa'], p['sb'], *consts)


def _rope_tables(start, length):
    inv_freq = 1.0 / (ROPE_THETA ** (jnp.arange(0, MLA_D_ROPE, 2, dtype=F32) / MLA_D_ROPE))
    ang = jnp.arange(start, start + length, dtype=F32)[:, None] * inv_freq[None, :]
    cos, sin = jnp.cos(ang), jnp.sin(ang)
    zero = jnp.zeros_like(cos)
    return (jnp.concatenate([cos, cos, zero, zero], axis=1),
            jnp.concatenate([zero, sin, zero, zero], axis=1),
            jnp.concatenate([-sin, zero, zero, zero], axis=1))


def _prepare_weights(emb_ln_g, emb_ln_b, w_in, b_merge, w_gla_gate_f, b_gla_gate_f, w_gla_gate_b, b_gla_gate_b,
                     gla_norm_g, w_o_gla, q_a_norm_g, w_q_b, kv_a_norm_g, w_kv_b, w_o_mla, w_out,
                     post_ln_g, post_ln_b):
    offs = [0]
    for s in IN_SPLITS:
        offs.append(offs[-1] + s)
    col = lambda i: w_in[0][:, offs[i]:offs[i + 1]]
    zcols = lambda n: jnp.zeros((D_MODEL, n), F32)
    w_sm = jnp.concatenate([col(4), col(5), zcols(LANES - 2 * GLA_GATE_RANK), col(6), col(7), col(8),
                            zcols(LANES - MLA_D_ROPE)], axis=1)
    w_gate = jnp.zeros((LANES, 2 * GLA_KEY_WIDTH), F32)
    w_gate = w_gate.at[:GLA_GATE_RANK, :GLA_KEY_WIDTH].set(w_gla_gate_f[0])
    w_gate = w_gate.at[GLA_GATE_RANK:2 * GLA_GATE_RANK, GLA_KEY_WIDTH:].set(w_gla_gate_b[0])
    wq = w_q_b[0].reshape(MLA_Q_RANK, MLA_HEADS, MLA_D_NOPE + MLA_D_ROPE)
    wq = jnp.pad(wq, ((0, 0), (0, 0), (0, MLA_QK_PAD - MLA_D_NOPE - MLA_D_ROPE)))
    wkv = w_kv_b[0].reshape(MLA_KV_RANK, MLA_HEADS, MLA_D_NOPE + MLA_D_V)
    wkv = jnp.concatenate([wkv[:, :, :MLA_D_NOPE].reshape(MLA_KV_RANK, -1),
                           wkv[:, :, MLA_D_NOPE:].reshape(MLA_KV_RANK, -1)], axis=1)
    r2 = lambda a: a.reshape(1, -1).astype(F32)
    return {
        'emb_g': r2(emb_ln_g), 'emb_b': r2(emb_ln_b),
        'w_qk': jnp.concatenate([col(0), col(1)], axis=1).astype(BF16),
        'w_v': col(2).astype(BF16), 'w_r': col(3).astype(BF16), 'w_z': col(9).astype(BF16),
        'w_a': col(10).astype(BF16), 'w_b': col(11).astype(BF16), 'w_sm': w_sm.astype(BF16),
        'b_ma': r2(b_merge[0][:D_MODEL]), 'b_mb': r2(b_merge[0][D_MODEL:]),
        'w_gate': w_gate.astype(BF16),
        'b_gate': r2(jnp.concatenate([b_gla_gate_f[0], b_gla_gate_b[0]])),
        'qn_g': r2(q_a_norm_g[0]), 'w_q': wq.reshape(MLA_Q_RANK, MLA_HEADS * MLA_QK_PAD).astype(BF16),
        'kvn_g': r2(kv_a_norm_g[0]), 'w_kv': wkv.astype(BF16),
        'gla_norm_g': r2(gla_norm_g[0]),
        'w_o_gla': w_o_gla[0].astype(BF16), 'w_o_mla': w_o_mla[0].astype(BF16), 'w_out': w_out[0].astype(BF16),
        'post_g': r2(post_ln_g[0]), 'post_b': r2(post_ln_b[0]),
    }


def _pick_tile(n, target):
    t = min(n, target)
    while n % t:
        t //= 2
    return t


def _encode(x, w, meta, s0, tm_in, tm_out, tq, tk):
    batch, seq, _ = x.shape
    x2d = x.reshape(batch * seq, D_MODEL)
    tm = _pick_tile(seq, tm_in)
    p = _inproj(x2d, tm, seq // tm, w, _rope_tables(N_META, seq))
    o_gla = _gla(p, s0, w['gla_norm_g'], batch, seq)
    o_mla = _attention(p, meta, batch, seq, _pick_tile(seq, tq), _pick_tile(seq, tk))
    y = _out_stage(x2d, p, o_gla, o_mla, w, _pick_tile(seq, tm_out))
    return y.reshape(batch, seq, D_MODEL)


def kernel(x_prompt, x_sample, meta_tokens, emb_ln_g, emb_ln_b, w_in, b_merge, w_gla_gate_f, b_gla_gate_f, w_gla_gate_b, b_gla_gate_b, gla_norm_g, w_o_gla, q_a_norm_g, w_q_b, kv_a_norm_g, w_kv_b, w_o_mla, w_out, post_ln_g, post_ln_b):
    w = _prepare_weights(emb_ln_g, emb_ln_b, w_in, b_merge, w_gla_gate_f, b_gla_gate_f, w_gla_gate_b,
                         b_gla_gate_b, gla_norm_g, w_o_gla, q_a_norm_g, w_q_b, kv_a_norm_g, w_kv_b, w_o_mla,
                         w_out, post_ln_g, post_ln_b)
    meta = _inproj(meta_tokens.astype(F32), N_META, 1, w, _rope_tables(0, N_META))
    lead = ((GLA_CHUNK - N_META, 0), (0, 0))
    s0 = _gla_meta_state(jnp.pad(meta['gk'], lead), jnp.pad(meta['gv'], lead), jnp.pad(meta['gf'], lead))
    y_prompt = _encode(x_prompt, w, meta, s0, 256, 256, 256, 512)
    y_sample = _encode(x_sample, w, meta, s0, 256, 256, 256, 512)
    return (y_prompt, y_sample)
```

```python
import functools

import jax
import jax.numpy as jnp
from jax import lax
from jax.experimental import pallas as pl
from jax.experimental.pallas import tpu as pltpu

D_MODEL = 1024
N_META = 16
GLA_HEADS = 4
GLA_DK = 128
GLA_DV = 256
GLA_KEY_WIDTH = GLA_HEADS * GLA_DK
GLA_WIDTH = GLA_HEADS * GLA_DV
GLA_GATE_RANK = 16
GLA_TAU = 16.0
GLA_CHUNK = 64
MLA_HEADS = 8
MLA_D_NOPE = 128
MLA_D_ROPE = 64
MLA_D_V = 128
MLA_Q_RANK = 384
MLA_KV_RANK = 256
MLA_WIDTH = MLA_HEADS * MLA_D_V
ROPE_THETA = 10000.0
LN_EPS = 1e-5
RMS_EPS = 1e-6
DEPTH = 1
DEEPNORM_ALPHA = (2 * DEPTH) ** 0.25
IN_SPLITS = (GLA_KEY_WIDTH, GLA_KEY_WIDTH, GLA_WIDTH, GLA_WIDTH, GLA_GATE_RANK, GLA_GATE_RANK,
             MLA_Q_RANK, MLA_KV_RANK, MLA_D_ROPE, MLA_WIDTH, D_MODEL, D_MODEL)

LANES = 128
MLA_QK_PAD = 2 * LANES
BF16_SUBLANES = 16
MLA_V_EXT = MLA_D_V + BF16_SUBLANES
SMALL_COLS = LANES + MLA_Q_RANK + MLA_KV_RANK + LANES
VMEM_LIMIT = 56 * 1024 * 1024

LOG2_E = 1.4426950408889634
ATTN_UNROLL = 4

F32 = jnp.float32
BF16 = jnp.bfloat16
NT_DIMS = (((1,), (1,)), ((), ()))
TN_DIMS = (((0,), (0,)), ((), ()))


def _dot(a, b):
    return jnp.dot(a, b, preferred_element_type=F32)


def _sigmoid(x):
    return 1.0 / (1.0 + jnp.exp(-x))


def _layer_norm(x, g, b):
    mu = jnp.mean(x, axis=-1, keepdims=True)
    xc = x - mu
    var = jnp.mean(xc * xc, axis=-1, keepdims=True)
    return xc * lax.rsqrt(var + LN_EPS) * g + b


def _rms_norm(x, g):
    ms = jnp.mean(x * x, axis=-1, keepdims=True)
    return x * lax.rsqrt(ms + RMS_EPS) * g


def _rope(blk, c, s1, s2):
    return blk * c + pltpu.roll(blk, 32, 1) * s1 + pltpu.roll(blk, 96, 1) * s2


def _inproj_kernel(x_ref, lng_ref, lnb_ref, wqk_ref, wv_ref, wr_ref, wz_ref, wa_ref, wb_ref, wsm_ref,
                   bma_ref, bmb_ref, wgate_ref, bgate_ref, qng_ref, wq_ref, kvng_ref, wkv_ref,
                   rc_ref, rs1_ref, rs2_ref,
                   gq_ref, gk_ref, gv_ref, gr_ref, gf_ref, gb_ref, q_ref, k_ref, v_ref, mz_ref,
                   sa_ref, sb_ref, *, transposed):
    h = _layer_norm(x_ref[...], lng_ref[...], lnb_ref[...])
    hb = h.astype(BF16)

    p = _dot(hb, wqk_ref[...])
    gq_ref[...] = (p[:, :GLA_KEY_WIDTH] * (GLA_DK ** -0.5)).astype(BF16)
    gk_ref[...] = p[:, GLA_KEY_WIDTH:].astype(BF16)
    gv_ref[...] = _dot(hb, wv_ref[...]).astype(BF16)
    r = _dot(hb, wr_ref[...])
    gr_ref[...] = (r * _sigmoid(r)).astype(BF16)
    z = _dot(hb, wz_ref[...])
    mz_ref[...] = (z * _sigmoid(z)).astype(BF16)
    sa_ref[...] = _sigmoid(_dot(hb, wa_ref[...]) + bma_ref[...]).astype(BF16)
    sb_ref[...] = _sigmoid(_dot(hb, wb_ref[...]) + bmb_ref[...]).astype(BF16)

    p = _dot(hb, wsm_ref[...])
    gx = _dot(p[:, :LANES].astype(BF16), wgate_ref[...]) + bgate_ref[...]
    ls = (jnp.minimum(gx, 0.0) - jnp.log1p(jnp.exp(-jnp.abs(gx)))) * (1.0 / GLA_TAU)
    gf_ref[...] = ls[:, :GLA_KEY_WIDTH]
    gb_ref[...] = ls[:, GLA_KEY_WIDTH:]

    rc, rs1, rs2 = rc_ref[...], rs1_ref[...], rs2_ref[...]
    scale = (MLA_D_NOPE + MLA_D_ROPE) ** -0.5 * LOG2_E
    cq = _rms_norm(p[:, LANES:LANES + MLA_Q_RANK], qng_ref[...]).astype(BF16)
    qf = _dot(cq, wq_ref[...])
    for hd in range(MLA_HEADS):
        lo = hd * MLA_QK_PAD
        q_nope = qf[:, lo:lo + LANES] * scale
        q_rope = _rope(qf[:, lo + LANES:lo + MLA_QK_PAD], rc, rs1, rs2) * scale
        if transposed:
            q_ref[hd, 0, :LANES, :] = q_nope.T.astype(BF16)
            q_ref[hd, 0, LANES:, :] = q_rope.T.astype(BF16)
        else:
            q_ref[hd, :, :LANES] = q_nope.astype(BF16)
            q_ref[hd, :, LANES:] = q_rope.astype(BF16)

    kv_lo = LANES + MLA_Q_RANK
    ckv = _rms_norm(p[:, kv_lo:kv_lo + MLA_KV_RANK], kvng_ref[...]).astype(BF16)
    kvf = _dot(ckv, wkv_ref[...])
    kr = _rope(p[:, kv_lo + MLA_KV_RANK:], rc, rs1, rs2).astype(BF16)
    for hd in range(MLA_HEADS):
        k_ref[hd, :, :LANES] = kvf[:, hd * LANES:(hd + 1) * LANES].astype(BF16)
        k_ref[hd, :, LANES:] = kr
    v_lo = MLA_HEADS * MLA_D_NOPE
    if transposed:
        tm = x_ref.shape[0]
        ones_row = (lax.broadcasted_iota(jnp.int32, (MLA_V_EXT - MLA_D_V, tm), 0) == 0).astype(BF16)
        for hd in range(MLA_HEADS):
            v_ref[hd, 0, :MLA_D_V, :] = kvf[:, v_lo + hd * MLA_D_V:v_lo + (hd + 1) * MLA_D_V].T.astype(BF16)
            v_ref[hd, 0, MLA_D_V:, :] = ones_row
    else:
        v_ref[...] = kvf[:, v_lo:].astype(BF16)


def _const_spec(shape):
    nd = len(shape)
    return pl.BlockSpec(shape, lambda i: (0,) * nd, pipeline_mode=pl.Buffered(1))


def _inproj(x2d, tm, seq_tiles, w, rope_tabs, transposed):
    n = x2d.shape[0]
    row = lambda cols: pl.BlockSpec((tm, cols), lambda i: (i, 0))
    tab = pl.BlockSpec((tm, LANES), lambda i: (i % seq_tiles, 0))
    consts = [w['emb_g'], w['emb_b'], w['w_qk'], w['w_v'], w['w_r'], w['w_z'], w['w_a'], w['w_b'], w['w_sm'],
              w['b_ma'], w['b_mb'], w['w_gate'], w['b_gate'], w['qn_g'], w['w_q'], w['kvn_g'], w['w_kv']]
    in_specs = [row(D_MODEL)] + [_const_spec(c.shape) for c in consts] + [tab, tab, tab]
    head_spec = pl.BlockSpec((MLA_HEADS, tm, MLA_QK_PAD), lambda i: (0, i, 0))
    head_shape = jax.ShapeDtypeStruct((MLA_HEADS, n, MLA_QK_PAD), BF16)
    if transposed:
        q_spec = pl.BlockSpec((MLA_HEADS, 1, MLA_QK_PAD, tm), lambda i: (0, i, 0, 0))
        q_shape = jax.ShapeDtypeStruct((MLA_HEADS, n // tm, MLA_QK_PAD, tm), BF16)
        v_spec = pl.BlockSpec((MLA_HEADS, 1, MLA_V_EXT, tm), lambda i: (0, i, 0, 0))
        v_shape = jax.ShapeDtypeStruct((MLA_HEADS, n // tm, MLA_V_EXT, tm), BF16)
    else:
        q_spec, q_shape = head_spec, head_shape
        v_spec, v_shape = row(MLA_WIDTH), jax.ShapeDtypeStruct((n, MLA_WIDTH), BF16)
    out_shape = [
        jax.ShapeDtypeStruct((n, GLA_KEY_WIDTH), BF16), jax.ShapeDtypeStruct((n, GLA_KEY_WIDTH), BF16),
        jax.ShapeDtypeStruct((n, GLA_WIDTH), BF16), jax.ShapeDtypeStruct((n, GLA_WIDTH), BF16),
        jax.ShapeDtypeStruct((n, GLA_KEY_WIDTH), F32), jax.ShapeDtypeStruct((n, GLA_KEY_WIDTH), F32),
        q_shape, head_shape, v_shape, jax.ShapeDtypeStruct((n, MLA_WIDTH), BF16),
        jax.ShapeDtypeStruct((n, D_MODEL), BF16), jax.ShapeDtypeStruct((n, D_MODEL), BF16),
    ]
    out_specs = [row(GLA_KEY_WIDTH), row(GLA_KEY_WIDTH), row(GLA_WIDTH), row(GLA_WIDTH),
                 row(GLA_KEY_WIDTH), row(GLA_KEY_WIDTH), q_spec, head_spec,
                 v_spec, row(MLA_WIDTH), row(D_MODEL), row(D_MODEL)]
    outs = pl.pallas_call(
        functools.partial(_inproj_kernel, transposed=transposed),
        grid=(n // tm,),
        in_specs=in_specs,
        out_specs=out_specs,
        out_shape=out_shape,
        compiler_params=pltpu.CompilerParams(dimension_semantics=("arbitrary",), vmem_limit_bytes=VMEM_LIMIT),
        name="inproj",
    )(x2d, *consts, *rope_tabs)
    names = ('gq', 'gk', 'gv', 'gr', 'gf', 'gb', 'q', 'k', 'v', 'mz', 'sa', 'sb')
    return dict(zip(names, outs))


def _chunk_cumsum(tri, g):
    g_hi = g.astype(BF16)
    g_lo = (g - g_hi.astype(F32)).astype(BF16)
    return _dot(tri, g_hi) + _dot(tri, g_lo)


def _decay_columns(b_last):
    d = jnp.transpose(jnp.broadcast_to(jnp.exp(b_last), (GLA_DK, GLA_DK)))
    return jnp.concatenate([d, d], axis=1)


def _gla_chunk(q, k, v, g, state, tri, mask, mid_row, last_row):
    bc = _chunk_cumsum(tri, g)
    b_mid = bc[mid_row:mid_row + 1]
    b_last = bc[last_row:last_row + 1]
    qa = (q * jnp.exp(bc - b_mid)).astype(BF16)
    ka = (k * jnp.exp(b_mid - bc)).astype(BF16)
    a = lax.dot_general(qa, ka, NT_DIMS, preferred_element_type=F32)
    a = jnp.where(mask, a, 0.0).astype(BF16)
    q_in = (q * jnp.exp(bc)).astype(BF16)
    o = _dot(a, v) + _dot(q_in, state.astype(BF16))
    k_dec = (k * jnp.exp(b_last - bc)).astype(BF16)
    new_state = state * _decay_columns(b_last) + lax.dot_general(k_dec, v, TN_DIMS, preferred_element_type=F32)
    return o, new_state


def _chunk_consts():
    c = GLA_CHUNK
    r = lax.broadcasted_iota(jnp.int32, (c, c), 0)
    s = lax.broadcasted_iota(jnp.int32, (c, c), 1)
    lower = r >= s
    upper = r <= s
    return (jnp.where(lower, 1.0, 0.0).astype(BF16), lower,
            jnp.where(upper, 1.0, 0.0).astype(BF16), r < s)


def _gla_kernel(q_ref, k_ref, v_ref, gf_ref, gb_ref, s0_ref, ng_ref, o_ref, acc_ref, sf_ref, sb_ref, *, n_chunks):
    c = GLA_CHUNK
    tri_f, mask_f, tri_b, mask_b = _chunk_consts()
    acc_ref[...] = jnp.zeros_like(acc_ref)
    sf_ref[...] = s0_ref[0]
    sb_ref[...] = jnp.zeros_like(sb_ref)

    def body(i, carry):
        rf = pl.multiple_of(i * c, c)
        rb = pl.multiple_of((n_chunks - 1 - i) * c, c)
        o_f, s_f = _gla_chunk(q_ref[pl.ds(rf, c), :].astype(F32), k_ref[pl.ds(rf, c), :].astype(F32),
                              v_ref[pl.ds(rf, c), :], gf_ref[pl.ds(rf, c), :], sf_ref[...],
                              tri_f, mask_f, c // 2, c - 1)
        sf_ref[...] = s_f
        acc_ref[pl.ds(rf, c), :] += o_f
        o_b, s_b = _gla_chunk(q_ref[pl.ds(rb, c), :].astype(F32), k_ref[pl.ds(rb, c), :].astype(F32),
                              v_ref[pl.ds(rb, c), :], gb_ref[pl.ds(rb, c), :], sb_ref[...],
                              tri_b, mask_b, c // 2 - 1, 0)
        sb_ref[...] = s_b
        acc_ref[pl.ds(rb, c), :] += o_b
        return carry

    lax.fori_loop(0, n_chunks, body, 0)

    blk = 4 * c

    def norm_body(i, carry):
        r0 = pl.multiple_of(i * blk, blk)
        o_ref[pl.ds(r0, blk), :] = _rms_norm(acc_ref[pl.ds(r0, blk), :], ng_ref[...]).astype(BF16)
        return carry

    lax.fori_loop(0, n_chunks * c // blk, norm_body, 0)


def _gla(p, s0, norm_g, batch, seq):
    n_chunks = seq // GLA_CHUNK
    kspec = pl.BlockSpec((seq, GLA_DK), lambda b, h: (b, h))
    vspec = pl.BlockSpec((seq, GLA_DV), lambda b, h: (b, h))
    return pl.pallas_call(
        functools.partial(_gla_kernel, n_chunks=n_chunks),
        grid=(batch, GLA_HEADS),
        in_specs=[kspec, kspec, vspec, kspec, kspec,
                  pl.BlockSpec((1, GLA_DK, GLA_DV), lambda b, h: (h, 0, 0)),
                  pl.BlockSpec((1, GLA_DV), lambda b, h: (0, 0))],
        out_specs=vspec,
        out_shape=jax.ShapeDtypeStruct((batch * seq, GLA_WIDTH), BF16),
        scratch_shapes=[pltpu.VMEM((seq, GLA_DV), F32), pltpu.VMEM((GLA_DK, GLA_DV), F32),
                        pltpu.VMEM((GLA_DK, GLA_DV), F32)],
        compiler_params=pltpu.CompilerParams(dimension_semantics=("arbitrary", "arbitrary"),
                                             vmem_limit_bytes=VMEM_LIMIT),
        name="gla",
    )(p['gq'], p['gk'], p['gv'], p['gf'], p['gb'], s0, norm_g)


def _gla_meta_state_kernel(k_ref, v_ref, g_ref, s_ref):
    tri_f, _, _, _ = _chunk_consts()
    bc = _chunk_cumsum(tri_f, g_ref[...])
    k_dec = (k_ref[...].astype(F32) * jnp.exp(bc[GLA_CHUNK - 1:GLA_CHUNK] - bc)).astype(BF16)
    s_ref[0] = lax.dot_general(k_dec, v_ref[...], TN_DIMS, preferred_element_type=F32)


def _gla_meta_state(k_pad, v_pad, g_pad):
    return pl.pallas_call(
        _gla_meta_state_kernel,
        grid=(GLA_HEADS,),
        in_specs=[pl.BlockSpec((GLA_CHUNK, GLA_DK), lambda h: (0, h)),
                  pl.BlockSpec((GLA_CHUNK, GLA_DV), lambda h: (0, h)),
                  pl.BlockSpec((GLA_CHUNK, GLA_DK), lambda h: (0, h))],
        out_specs=pl.BlockSpec((1, GLA_DK, GLA_DV), lambda h: (h, 0, 0)),
        out_shape=jax.ShapeDtypeStruct((GLA_HEADS, GLA_DK, GLA_DV), F32),
        name="gla_meta_state",
    )(k_pad, v_pad, g_pad)


def _attn_kernel(qt_ref, k_ref, vt_ref, km_ref, vmt_ref, z_ref, o_ref, acc_ref, m0_ref, s_ref, p_ref, *,
                 tk, n_q, n_kv, unroll):
    n_split = acc_ref.shape[1]
    hw = acc_ref.shape[3]
    kc = tk // hw
    n_tiles = n_q * n_kv

    def stage_a(t, slot):
        qi, j = t // n_kv, t % n_kv
        k_j = k_ref[0, pl.ds(pl.multiple_of(j * tk, tk), tk), :]
        tile_max = []
        for a in range(n_split):
            s = _dot(k_j, qt_ref[0, qi * n_split + a])
            s_ref[slot, a] = s
            tile_max.append(jnp.max(s, axis=0, keepdims=True))
        return tuple(tile_max)

    def stage_b(t, slot, tile_max, ms):
        qi, j = t // n_kv, t % n_kv
        new_ms, alphas = [], []
        for a in range(n_split):
            m_old = jnp.where(j == 0, m0_ref[qi, a], ms[a])
            m_new = jnp.maximum(m_old, tile_max[a])
            alphas.append(jnp.exp2(m_old - m_new))
            p_ref[slot, a] = jnp.exp2(s_ref[slot, a] - m_new).astype(BF16)
            new_ms.append(m_new)
        return tuple(new_ms), tuple(alphas)

    def stage_c(t, slot, alphas):
        qi, j = t // n_kv, t % n_kv
        for a in range(n_split):
            pv = _dot(vt_ref[0, j * kc], p_ref[slot, a, :hw, :])
            for c in range(1, kc):
                pv += _dot(vt_ref[0, j * kc + c], p_ref[slot, a, c * hw:(c + 1) * hw, :])
            acc_ref[qi, a] = alphas[a] * acc_ref[qi, a] + pv

    def step(t, parity, carry):
        ms, alphas, max_next = carry
        ms, new_alphas = stage_b(t + 1, 1 - parity, max_next, ms)
        max_after = stage_a(t + 2, parity)
        stage_c(t, parity, alphas)
        return ms, new_alphas, max_after

    groups = range(n_q * n_split)
    s_meta = [_dot(km_ref[0], qt_ref[0, g]) for g in groups]
    m_meta = [jnp.max(s, axis=0, keepdims=True) for s in s_meta]
    p_meta = [jnp.exp2(s - m).astype(BF16) for s, m in zip(s_meta, m_meta)]
    for g in groups:
        m0_ref[g // n_split, g % n_split] = m_meta[g]
        acc_ref[g // n_split, g % n_split] = _dot(vmt_ref[0], p_meta[g])

    max_next = stage_a(0, 0)
    ms, alphas = stage_b(0, 0, max_next, tuple(m_meta[:n_split]))
    carry = (ms, alphas, stage_a(1, 1))

    n_steps = n_tiles - 2
    n_loops = n_steps // unroll

    def body(i, carry):
        for u in range(unroll):
            carry = step(i * unroll + u, u % 2, carry)
        return carry

    carry = lax.fori_loop(0, n_loops, body, carry)
    for t in range(n_loops * unroll, n_steps):
        carry = step(t, t % 2, carry)
    ms, alphas, max_next = carry
    stage_c(n_tiles - 2, n_tiles % 2, alphas)
    ms, alphas = stage_b(n_tiles - 1, (n_tiles - 1) % 2, max_next, ms)
    stage_c(n_tiles - 1, (n_tiles - 1) % 2, alphas)

    for g in groups:
        acc = acc_ref[g // n_split, g % n_split]
        o = jnp.transpose(acc[:MLA_D_V] * (1.0 / acc[MLA_D_V:MLA_D_V + 1]))
        rows = slice(g * hw, (g + 1) * hw)
        o_ref[rows, :] = (o * z_ref[rows, :].astype(F32)).astype(BF16)


def _attention(p, meta, batch, seq, tq, tk):
    hw = p['q'].shape[3]
    n_split, n_q, n_kv = tq // hw, seq // tq, seq // tk
    assert tq % hw == 0 and tk % hw == 0 and n_q * n_kv >= 3
    return pl.pallas_call(
        functools.partial(_attn_kernel, tk=tk, n_q=n_q, n_kv=n_kv, unroll=ATTN_UNROLL),
        grid=(batch, MLA_HEADS),
        in_specs=[pl.BlockSpec((1, seq // hw, MLA_QK_PAD, hw), lambda b, h: (h, b, 0, 0)),
                  pl.BlockSpec((1, seq, MLA_QK_PAD), lambda b, h: (h, b, 0)),
                  pl.BlockSpec((1, seq // hw, MLA_V_EXT, hw), lambda b, h: (h, b, 0, 0)),
                  pl.BlockSpec((1, N_META, MLA_QK_PAD), lambda b, h: (h, 0, 0)),
                  pl.BlockSpec((1, MLA_V_EXT, N_META), lambda b, h: (h, 0, 0)),
                  pl.BlockSpec((seq, MLA_D_V), lambda b, h: (b, h))],
        out_specs=pl.BlockSpec((seq, MLA_D_V), lambda b, h: (b, h)),
        out_shape=jax.ShapeDtypeStruct((batch * seq, MLA_WIDTH), BF16),
        scratch_shapes=[pltpu.VMEM((n_q, n_split, MLA_V_EXT, hw), F32),
                        pltpu.VMEM((n_q, n_split, 1, hw), F32),
                        pltpu.VMEM((2, n_split, tk, hw), F32),
                        pltpu.VMEM((2, n_split, tk, hw), BF16)],
        compiler_params=pltpu.CompilerParams(dimension_semantics=("arbitrary", "arbitrary"),
                                             vmem_limit_bytes=VMEM_LIMIT),
        name="mla_attention",
    )(p['q'], p['k'], p['v'], meta['k'], meta['vt'], p['mz'])


def _out_kernel(x_ref, og_ref, gr_ref, om_ref, sa_ref, sb_ref, lng_ref, lnb_ref, woa_ref, wob_ref, wout_ref,
                png_ref, pnb_ref, y_ref):
    h = _layer_norm(x_ref[...], lng_ref[...], lnb_ref[...])
    a_in = (og_ref[...].astype(F32) * gr_ref[...].astype(F32)).astype(BF16)
    branch_a = _dot(a_in, woa_ref[...])
    branch_b = _dot(om_ref[...], wob_ref[...])
    mixed = sa_ref[...].astype(F32) * branch_a + sb_ref[...].astype(F32) * branch_b
    out = _dot(mixed.astype(BF16), wout_ref[...])
    y_ref[...] = _layer_norm(DEEPNORM_ALPHA * h + out, png_ref[...], pnb_ref[...])


def _out_stage(x2d, p, o_gla, o_mla, w, tm):
    n = x2d.shape[0]
    row = pl.BlockSpec((tm, D_MODEL), lambda i: (i, 0))
    consts = [w['emb_g'], w['emb_b'], w['w_o_gla'], w['w_o_mla'], w['w_out'], w['post_g'], w['post_b']]
    return pl.pallas_call(
        _out_kernel,
        grid=(n // tm,),
        in_specs=[row] * 6 + [_const_spec(c.shape) for c in consts],
        out_specs=row,
        out_shape=jax.ShapeDtypeStruct((n, D_MODEL), F32),
        compiler_params=pltpu.CompilerParams(dimension_semantics=("arbitrary",), vmem_limit_bytes=VMEM_LIMIT),
        name="merge_out",
    )(x2d, o_gla, p['gr'], o_mla, p['sa'], p['sb'], *consts)


def _rope_tables(start, length):
    inv_freq = 1.0 / (ROPE_THETA ** (jnp.arange(0, MLA_D_ROPE, 2, dtype=F32) / MLA_D_ROPE))
    ang = jnp.arange(start, start + length, dtype=F32)[:, None] * inv_freq[None, :]
    cos, sin = jnp.cos(ang), jnp.sin(ang)
    zero = jnp.zeros_like(cos)
    return (jnp.concatenate([cos, cos, zero, zero], axis=1),
            jnp.concatenate([zero, sin, zero, zero], axis=1),
            jnp.concatenate([-sin, zero, zero, zero], axis=1))


def _prepare_weights(emb_ln_g, emb_ln_b, w_in, b_merge, w_gla_gate_f, b_gla_gate_f, w_gla_gate_b, b_gla_gate_b,
                     gla_norm_g, w_o_gla, q_a_norm_g, w_q_b, kv_a_norm_g, w_kv_b, w_o_mla, w_out,
                     post_ln_g, post_ln_b):
    offs = [0]
    for s in IN_SPLITS:
        offs.append(offs[-1] + s)
    col = lambda i: w_in[0][:, offs[i]:offs[i + 1]]
    zcols = lambda n: jnp.zeros((D_MODEL, n), F32)
    w_sm = jnp.concatenate([col(4), col(5), zcols(LANES - 2 * GLA_GATE_RANK), col(6), col(7), col(8),
                            zcols(LANES - MLA_D_ROPE)], axis=1)
    w_gate = jnp.zeros((LANES, 2 * GLA_KEY_WIDTH), F32)
    w_gate = w_gate.at[:GLA_GATE_RANK, :GLA_KEY_WIDTH].set(w_gla_gate_f[0])
    w_gate = w_gate.at[GLA_GATE_RANK:2 * GLA_GATE_RANK, GLA_KEY_WIDTH:].set(w_gla_gate_b[0])
    wq = w_q_b[0].reshape(MLA_Q_RANK, MLA_HEADS, MLA_D_NOPE + MLA_D_ROPE)
    wq = jnp.pad(wq, ((0, 0), (0, 0), (0, MLA_QK_PAD - MLA_D_NOPE - MLA_D_ROPE)))
    wkv = w_kv_b[0].reshape(MLA_KV_RANK, MLA_HEADS, MLA_D_NOPE + MLA_D_V)
    wkv = jnp.concatenate([wkv[:, :, :MLA_D_NOPE].reshape(MLA_KV_RANK, -1),
                           wkv[:, :, MLA_D_NOPE:].reshape(MLA_KV_RANK, -1)], axis=1)
    r2 = lambda a: a.reshape(1, -1).astype(F32)
    return {
        'emb_g': r2(emb_ln_g), 'emb_b': r2(emb_ln_b),
        'w_qk': jnp.concatenate([col(0), col(1)], axis=1).astype(BF16),
        'w_v': col(2).astype(BF16), 'w_r': col(3).astype(BF16), 'w_z': col(9).astype(BF16),
        'w_a': col(10).astype(BF16), 'w_b': col(11).astype(BF16), 'w_sm': w_sm.astype(BF16),
        'b_ma': r2(b_merge[0][:D_MODEL]), 'b_mb': r2(b_merge[0][D_MODEL:]),
        'w_gate': w_gate.astype(BF16),
        'b_gate': r2(jnp.concatenate([b_gla_gate_f[0], b_gla_gate_b[0]])),
        'qn_g': r2(q_a_norm_g[0]), 'w_q': wq.reshape(MLA_Q_RANK, MLA_HEADS * MLA_QK_PAD).astype(BF16),
        'kvn_g': r2(kv_a_norm_g[0]), 'w_kv': wkv.astype(BF16),
        'gla_norm_g': r2(gla_norm_g[0]),
        'w_o_gla': w_o_gla[0].astype(BF16), 'w_o_mla': w_o_mla[0].astype(BF16), 'w_out': w_out[0].astype(BF16),
        'post_g': r2(post_ln_g[0]), 'post_b': r2(post_ln_b[0]),
    }


def _pick_tile(n, target):
    t = min(n, target)
    while n % t:
        t //= 2
    return t


def _encode(x, w, meta, s0, tm_in, tm_out, tq, tk):
    batch, seq, _ = x.shape
    x2d = x.reshape(batch * seq, D_MODEL)
    tm = _pick_tile(seq, tm_in)
    p = _inproj(x2d, tm, seq // tm, w, _rope_tables(N_META, seq), True)
    o_gla = _gla(p, s0, w['gla_norm_g'], batch, seq)
    o_mla = _attention(p, meta, batch, seq, _pick_tile(seq, tq), _pick_tile(seq, tk))
    y = _out_stage(x2d, p, o_gla, o_mla, w, _pick_tile(seq, tm_out))
    return y.reshape(batch, seq, D_MODEL)


def kernel(x_prompt, x_sample, meta_tokens, emb_ln_g, emb_ln_b, w_in, b_merge, w_gla_gate_f, b_gla_gate_f, w_gla_gate_b, b_gla_gate_b, gla_norm_g, w_o_gla, q_a_norm_g, w_q_b, kv_a_norm_g, w_kv_b, w_o_mla, w_out, post_ln_g, post_ln_b):
    w = _prepare_weights(emb_ln_g, emb_ln_b, w_in, b_merge, w_gla_gate_f, b_gla_gate_f, w_gla_gate_b,
                         b_gla_gate_b, gla_norm_g, w_o_gla, q_a_norm_g, w_q_b, kv_a_norm_g, w_kv_b, w_o_mla,
                         w_out, post_ln_g, post_ln_b)
    meta = _inproj(meta_tokens.astype(F32), N_META, 1, w, _rope_tables(0, N_META), False)
    vt = meta['v'].reshape(N_META, MLA_HEADS, MLA_D_V).transpose(1, 2, 0)
    ones_row = jnp.zeros((MLA_HEADS, MLA_V_EXT - MLA_D_V, N_META), BF16).at[:, 0, :].set(1.0)
    meta['vt'] = jnp.concatenate([vt, ones_row], axis=1)
    lead = ((GLA_CHUNK - N_META, 0), (0, 0))
    s0 = _gla_meta_state(jnp.pad(meta['gk'], lead), jnp.pad(meta['gv'], lead), jnp.pad(meta['gf'], lead))
    y_prompt = _encode(x_prompt, w, meta, s0, 256, 256, 512, 512)
    y_sample = _encode(x_sample, w, meta, s0, 256, 256, 512, 512)
    return (y_prompt, y_sample)
```

```python
import functools

import jax
import jax.numpy as jnp
from jax import lax
from jax.experimental import pallas as pl
from jax.experimental.pallas import tpu as pltpu

D_MODEL = 1024
N_META = 16
GLA_HEADS = 4
GLA_DK = 128
GLA_DV = 256
GLA_KEY_WIDTH = GLA_HEADS * GLA_DK
GLA_WIDTH = GLA_HEADS * GLA_DV
GLA_GATE_RANK = 16
GLA_TAU = 16.0
GLA_CHUNK = 64
MLA_HEADS = 8
MLA_D_NOPE = 128
MLA_D_ROPE = 64
MLA_D_V = 128
MLA_Q_RANK = 384
MLA_KV_RANK = 256
MLA_WIDTH = MLA_HEADS * MLA_D_V
ROPE_THETA = 10000.0
LN_EPS = 1e-5
RMS_EPS = 1e-6
DEPTH = 1
DEEPNORM_ALPHA = (2 * DEPTH) ** 0.25
IN_SPLITS = (GLA_KEY_WIDTH, GLA_KEY_WIDTH, GLA_WIDTH, GLA_WIDTH, GLA_GATE_RANK, GLA_GATE_RANK,
             MLA_Q_RANK, MLA_KV_RANK, MLA_D_ROPE, MLA_WIDTH, D_MODEL, D_MODEL)

LANES = 128
MLA_QK_PAD = 2 * LANES
BF16_SUBLANES = 16
MLA_V_EXT = MLA_D_V + BF16_SUBLANES
SMALL_COLS = LANES + MLA_Q_RANK + MLA_KV_RANK + LANES
VMEM_LIMIT = 56 * 1024 * 1024

LOG2_E = 1.4426950408889634
GLA_GROUP = 8
ATTN_UNROLL = 4

F32 = jnp.float32
BF16 = jnp.bfloat16
NT_DIMS = (((1,), (1,)), ((), ()))
TN_DIMS = (((0,), (0,)), ((), ()))


def _dot(a, b):
    return jnp.dot(a, b, preferred_element_type=F32)


def _sigmoid(x):
    return 1.0 / (1.0 + jnp.exp(-x))


def _layer_norm(x, g, b):
    mu = jnp.mean(x, axis=-1, keepdims=True)
    xc = x - mu
    var = jnp.mean(xc * xc, axis=-1, keepdims=True)
    return xc * lax.rsqrt(var + LN_EPS) * g + b


def _rms_norm(x, g):
    ms = jnp.mean(x * x, axis=-1, keepdims=True)
    return x * lax.rsqrt(ms + RMS_EPS) * g


def _rope(blk, c, s1, s2):
    return blk * c + pltpu.roll(blk, 32, 1) * s1 + pltpu.roll(blk, 96, 1) * s2


def _inproj_kernel(x_ref, lng_ref, lnb_ref, wqk_ref, wv_ref, wr_ref, wz_ref, wa_ref, wb_ref, wsm_ref,
                   bma_ref, bmb_ref, wgate_ref, bgate_ref, qng_ref, wq_ref, kvng_ref, wkv_ref,
                   rc_ref, rs1_ref, rs2_ref,
                   gq_ref, gk_ref, gv_ref, gr_ref, gf_ref, gb_ref, q_ref, k_ref, v_ref, mz_ref,
                   sa_ref, sb_ref, *, transposed):
    h = _layer_norm(x_ref[...], lng_ref[...], lnb_ref[...])
    hb = h.astype(BF16)

    p = _dot(hb, wqk_ref[...])
    gq_ref[...] = (p[:, :GLA_KEY_WIDTH] * (GLA_DK ** -0.5)).astype(BF16)
    gk_ref[...] = p[:, GLA_KEY_WIDTH:].astype(BF16)
    gv_ref[...] = _dot(hb, wv_ref[...]).astype(BF16)
    r = _dot(hb, wr_ref[...])
    gr_ref[...] = (r * _sigmoid(r)).astype(BF16)
    z = _dot(hb, wz_ref[...])
    mz_ref[...] = (z * _sigmoid(z)).astype(BF16)
    sa_ref[...] = _sigmoid(_dot(hb, wa_ref[...]) + bma_ref[...]).astype(BF16)
    sb_ref[...] = _sigmoid(_dot(hb, wb_ref[...]) + bmb_ref[...]).astype(BF16)

    p = _dot(hb, wsm_ref[...])
    gx = _dot(p[:, :LANES].astype(BF16), wgate_ref[...]) + bgate_ref[...]
    ls = (jnp.minimum(gx, 0.0) - jnp.log1p(jnp.exp(-jnp.abs(gx)))) * (1.0 / GLA_TAU)
    gf_ref[...] = ls[:, :GLA_KEY_WIDTH]
    gb_ref[...] = ls[:, GLA_KEY_WIDTH:]

    rc, rs1, rs2 = rc_ref[...], rs1_ref[...], rs2_ref[...]
    scale = (MLA_D_NOPE + MLA_D_ROPE) ** -0.5 * LOG2_E
    cq = _rms_norm(p[:, LANES:LANES + MLA_Q_RANK], qng_ref[...]).astype(BF16)
    qf = _dot(cq, wq_ref[...])
    for hd in range(MLA_HEADS):
        lo = hd * MLA_QK_PAD
        q_nope = qf[:, lo:lo + LANES] * scale
        q_rope = _rope(qf[:, lo + LANES:lo + MLA_QK_PAD], rc, rs1, rs2) * scale
        if transposed:
            q_ref[hd, 0, :LANES, :] = q_nope.T.astype(BF16)
            q_ref[hd, 0, LANES:, :] = q_rope.T.astype(BF16)
        else:
            q_ref[hd, :, :LANES] = q_nope.astype(BF16)
            q_ref[hd, :, LANES:] = q_rope.astype(BF16)

    kv_lo = LANES + MLA_Q_RANK
    ckv = _rms_norm(p[:, kv_lo:kv_lo + MLA_KV_RANK], kvng_ref[...]).astype(BF16)
    kvf = _dot(ckv, wkv_ref[...])
    kr = _rope(p[:, kv_lo + MLA_KV_RANK:], rc, rs1, rs2).astype(BF16)
    for hd in range(MLA_HEADS):
        k_ref[hd, :, :LANES] = kvf[:, hd * LANES:(hd + 1) * LANES].astype(BF16)
        k_ref[hd, :, LANES:] = kr
    v_lo = MLA_HEADS * MLA_D_NOPE
    if transposed:
        tm = x_ref.shape[0]
        ones_row = (lax.broadcasted_iota(jnp.int32, (MLA_V_EXT - MLA_D_V, tm), 0) == 0).astype(BF16)
        for hd in range(MLA_HEADS):
            v_ref[hd, 0, :MLA_D_V, :] = kvf[:, v_lo + hd * MLA_D_V:v_lo + (hd + 1) * MLA_D_V].T.astype(BF16)
            v_ref[hd, 0, MLA_D_V:, :] = ones_row
    else:
        v_ref[...] = kvf[:, v_lo:].astype(BF16)


def _const_spec(shape):
    nd = len(shape)
    return pl.BlockSpec(shape, lambda i: (0,) * nd, pipeline_mode=pl.Buffered(1))


def _inproj(x2d, tm, seq_tiles, w, rope_tabs, transposed):
    n = x2d.shape[0]
    row = lambda cols: pl.BlockSpec((tm, cols), lambda i: (i, 0))
    tab = pl.BlockSpec((tm, LANES), lambda i: (i % seq_tiles, 0))
    consts = [w['emb_g'], w['emb_b'], w['w_qk'], w['w_v'], w['w_r'], w['w_z'], w['w_a'], w['w_b'], w['w_sm'],
              w['b_ma'], w['b_mb'], w['w_gate'], w['b_gate'], w['qn_g'], w['w_q'], w['kvn_g'], w['w_kv']]
    in_specs = [row(D_MODEL)] + [_const_spec(c.shape) for c in consts] + [tab, tab, tab]
    head_spec = pl.BlockSpec((MLA_HEADS, tm, MLA_QK_PAD), lambda i: (0, i, 0))
    head_shape = jax.ShapeDtypeStruct((MLA_HEADS, n, MLA_QK_PAD), BF16)
    if transposed:
        q_spec = pl.BlockSpec((MLA_HEADS, 1, MLA_QK_PAD, tm), lambda i: (0, i, 0, 0))
        q_shape = jax.ShapeDtypeStruct((MLA_HEADS, n // tm, MLA_QK_PAD, tm), BF16)
        v_spec = pl.BlockSpec((MLA_HEADS, 1, MLA_V_EXT, tm), lambda i: (0, i, 0, 0))
        v_shape = jax.ShapeDtypeStruct((MLA_HEADS, n // tm, MLA_V_EXT, tm), BF16)
    else:
        q_spec, q_shape = head_spec, head_shape
        v_spec, v_shape = row(MLA_WIDTH), jax.ShapeDtypeStruct((n, MLA_WIDTH), BF16)
    out_shape = [
        jax.ShapeDtypeStruct((n, GLA_KEY_WIDTH), BF16), jax.ShapeDtypeStruct((n, GLA_KEY_WIDTH), BF16),
        jax.ShapeDtypeStruct((n, GLA_WIDTH), BF16), jax.ShapeDtypeStruct((n, GLA_WIDTH), BF16),
        jax.ShapeDtypeStruct((n, GLA_KEY_WIDTH), F32), jax.ShapeDtypeStruct((n, GLA_KEY_WIDTH), F32),
        q_shape, head_shape, v_shape, jax.ShapeDtypeStruct((n, MLA_WIDTH), BF16),
        jax.ShapeDtypeStruct((n, D_MODEL), BF16), jax.ShapeDtypeStruct((n, D_MODEL), BF16),
    ]
    out_specs = [row(GLA_KEY_WIDTH), row(GLA_KEY_WIDTH), row(GLA_WIDTH), row(GLA_WIDTH),
                 row(GLA_KEY_WIDTH), row(GLA_KEY_WIDTH), q_spec, head_spec,
                 v_spec, row(MLA_WIDTH), row(D_MODEL), row(D_MODEL)]
    outs = pl.pallas_call(
        functools.partial(_inproj_kernel, transposed=transposed),
        grid=(n // tm,),
        in_specs=in_specs,
        out_specs=out_specs,
        out_shape=out_shape,
        compiler_params=pltpu.CompilerParams(dimension_semantics=("arbitrary",), vmem_limit_bytes=VMEM_LIMIT),
        name="inproj",
    )(x2d, *consts, *rope_tabs)
    names = ('gq', 'gk', 'gv', 'gr', 'gf', 'gb', 'q', 'k', 'v', 'mz', 'sa', 'sb')
    return dict(zip(names, outs))


def _chunk_cumsum(tri, g):
    g_hi = g.astype(BF16)
    g_lo = (g - g_hi.astype(F32)).astype(BF16)
    r = _dot(tri, jnp.concatenate([g_hi, g_lo], axis=1))
    return r[:, :GLA_DK] + r[:, GLA_DK:]


def _decay_columns(b_last):
    d = jnp.transpose(jnp.broadcast_to(jnp.exp(b_last), (GLA_DK, GLA_DK)))
    return jnp.concatenate([d, d], axis=1)


def _chunk_consts():
    c = GLA_CHUNK
    r = lax.broadcasted_iota(jnp.int32, (c, c), 0)
    s = lax.broadcasted_iota(jnp.int32, (c, c), 1)
    rm = lax.broadcasted_iota(jnp.int32, (c, GLA_DK), 0)
    sm = lax.broadcasted_iota(jnp.int32, (c, GLA_DK), 1)
    return ((jnp.where(r >= s, 1.0, 0.0).astype(BF16), rm >= sm),
            (jnp.where(r <= s, 1.0, 0.0).astype(BF16), (rm < sm) & (sm < c)))


def _gla_superstep(q_ref, k_ref, v_ref, items, states):
    c = GLA_CHUNK
    zero_k = jnp.zeros((GLA_DK - c, GLA_DK), BF16)
    zero_v = jnp.zeros((GLA_DK - c, GLA_DV), BF16)
    bcs = [_chunk_cumsum(it['tri'], it['g_ref'][pl.ds(it['rows'], c), :]) for it in items]
    pre = []
    for it, bc in zip(items, bcs):
        q = q_ref[pl.ds(it['rows'], c), :].astype(F32)
        k = k_ref[pl.ds(it['rows'], c), :].astype(F32)
        b_mid = bc[it['mid']:it['mid'] + 1]
        b_last = bc[it['last']:it['last'] + 1]
        qa = q * jnp.exp(bc - b_mid)
        ka = k * jnp.exp(b_mid - bc)
        pre.append(dict(
            qa=qa.astype(BF16),
            ka=jnp.concatenate([ka.astype(BF16), zero_k], axis=0),
            q_in=(qa * jnp.exp(b_mid)).astype(BF16),
            k_dec=(ka * jnp.exp(b_last - b_mid)).astype(BF16),
            dcol=_decay_columns(b_last),
            v=v_ref[pl.ds(it['rows'], c), :]))
    a_list = [lax.dot_general(p['qa'], p['ka'], NT_DIMS, preferred_element_type=F32) for p in pre]
    kv_list = [lax.dot_general(p['k_dec'], p['v'], TN_DIMS, preferred_element_type=F32) for p in pre]
    states = list(states)
    for it, p, a, kv in zip(items, pre, a_list, kv_list):
        s = states[it['dirn']]
        lhs = jnp.concatenate([p['q_in'], jnp.where(it['mask'], a, 0.0).astype(BF16)], axis=1)
        rhs = jnp.concatenate([s.astype(BF16), p['v'], zero_v], axis=0)
        it['out_ref'][pl.ds(it['rows'], c), :] = _dot(lhs, rhs)
        states[it['dirn']] = s * p['dcol'] + kv
    return states


def _gla_kernel(q_ref, k_ref, v_ref, gf_ref, gb_ref, s0_ref, ng_ref, o_ref, of_ref, ob_ref, sf_ref, sb_ref, *,
                n_chunks, group):
    c = GLA_CHUNK
    (tri_f, mask_f), (tri_b, mask_b) = _chunk_consts()
    sf_ref[...] = s0_ref[0]
    sb_ref[...] = jnp.zeros_like(sb_ref)

    def body(i, carry):
        items = []
        for u in range(group):
            items.append(dict(rows=pl.multiple_of((i * group + u) * c, c), g_ref=gf_ref, tri=tri_f, mask=mask_f,
                              mid=c // 2, last=c - 1, dirn=0, out_ref=of_ref))
        for u in range(group):
            items.append(dict(rows=pl.multiple_of((n_chunks - 1 - i * group - u) * c, c), g_ref=gb_ref, tri=tri_b,
                              mask=mask_b, mid=c // 2 - 1, last=0, dirn=1, out_ref=ob_ref))
        s_f, s_b = _gla_superstep(q_ref, k_ref, v_ref, items, [sf_ref[...], sb_ref[...]])
        sf_ref[...] = s_f
        sb_ref[...] = s_b
        return carry

    lax.fori_loop(0, n_chunks // group, body, 0)

    blk = 4 * c

    def norm_body(i, carry):
        r0 = pl.multiple_of(i * blk, blk)
        o = of_ref[pl.ds(r0, blk), :] + ob_ref[pl.ds(r0, blk), :]
        o_ref[pl.ds(r0, blk), :] = _rms_norm(o, ng_ref[...]).astype(BF16)
        return carry

    lax.fori_loop(0, n_chunks * c // blk, norm_body, 0)


def _gla(p, s0, norm_g, batch, seq):
    n_chunks = seq // GLA_CHUNK
    kspec = pl.BlockSpec((seq, GLA_DK), lambda b, h: (b, h))
    vspec = pl.BlockSpec((seq, GLA_DV), lambda b, h: (b, h))
    return pl.pallas_call(
        functools.partial(_gla_kernel, n_chunks=n_chunks, group=_pick_tile(n_chunks, GLA_GROUP)),
        grid=(batch, GLA_HEADS),
        in_specs=[kspec, kspec, vspec, kspec, kspec,
                  pl.BlockSpec((1, GLA_DK, GLA_DV), lambda b, h: (h, 0, 0)),
                  pl.BlockSpec((1, GLA_DV), lambda b, h: (0, 0))],
        out_specs=vspec,
        out_shape=jax.ShapeDtypeStruct((batch * seq, GLA_WIDTH), BF16),
        scratch_shapes=[pltpu.VMEM((seq, GLA_DV), F32), pltpu.VMEM((seq, GLA_DV), F32),
                        pltpu.VMEM((GLA_DK, GLA_DV), F32), pltpu.VMEM((GLA_DK, GLA_DV), F32)],
        compiler_params=pltpu.CompilerParams(dimension_semantics=("arbitrary", "arbitrary"),
                                             vmem_limit_bytes=VMEM_LIMIT),
        name="gla",
    )(p['gq'], p['gk'], p['gv'], p['gf'], p['gb'], s0, norm_g)


def _gla_meta_state_kernel(k_ref, v_ref, g_ref, s_ref):
    (tri_f, _), _ = _chunk_consts()
    bc = _chunk_cumsum(tri_f, g_ref[...])
    k_dec = (k_ref[...].astype(F32) * jnp.exp(bc[GLA_CHUNK - 1:GLA_CHUNK] - bc)).astype(BF16)
    s_ref[0] = lax.dot_general(k_dec, v_ref[...], TN_DIMS, preferred_element_type=F32)


def _gla_meta_state(k_pad, v_pad, g_pad):
    return pl.pallas_call(
        _gla_meta_state_kernel,
        grid=(GLA_HEADS,),
        in_specs=[pl.BlockSpec((GLA_CHUNK, GLA_DK), lambda h: (0, h)),
                  pl.BlockSpec((GLA_CHUNK, GLA_DV), lambda h: (0, h)),
                  pl.BlockSpec((GLA_CHUNK, GLA_DK), lambda h: (0, h))],
        out_specs=pl.BlockSpec((1, GLA_DK, GLA_DV), lambda h: (h, 0, 0)),
        out_shape=jax.ShapeDtypeStruct((GLA_HEADS, GLA_DK, GLA_DV), F32),
        name="gla_meta_state",
    )(k_pad, v_pad, g_pad)


def _attn_kernel(qt_ref, k_ref, vt_ref, km_ref, vmt_ref, z_ref, o_ref, acc_ref, m0_ref, s_ref, p_ref, *,
                 tk, n_q, n_kv, unroll):
    n_split = acc_ref.shape[1]
    hw = acc_ref.shape[3]
    kc = tk // hw
    n_tiles = n_q * n_kv

    def stage_a(t, slot):
        qi, j = t // n_kv, t % n_kv
        k_j = k_ref[0, pl.ds(pl.multiple_of(j * tk, tk), tk), :]
        tile_max = []
        for a in range(n_split):
            s = _dot(k_j, qt_ref[0, qi * n_split + a])
            s_ref[slot, a] = s
            tile_max.append(jnp.max(s, axis=0, keepdims=True))
        return tuple(tile_max)

    def stage_b(t, slot, tile_max, ms):
        qi, j = t // n_kv, t % n_kv
        new_ms, alphas = [], []
        for a in range(n_split):
            m_old = jnp.where(j == 0, m0_ref[qi, a], ms[a])
            m_new = jnp.maximum(m_old, tile_max[a])
            alphas.append(jnp.exp2(m_old - m_new))
            p_ref[slot, a] = jnp.exp2(s_ref[slot, a] - m_new).astype(BF16)
            new_ms.append(m_new)
        return tuple(new_ms), tuple(alphas)

    def stage_c(t, slot, alphas):
        qi, j = t // n_kv, t % n_kv
        for a in range(n_split):
            pv = _dot(vt_ref[0, j * kc], p_ref[slot, a, :hw, :])
            for c in range(1, kc):
                pv += _dot(vt_ref[0, j * kc + c], p_ref[slot, a, c * hw:(c + 1) * hw, :])
            acc_ref[qi, a] = alphas[a] * acc_ref[qi, a] + pv

    def step(t, parity, carry):
        ms, alphas, max_next = carry
        ms, new_alphas = stage_b(t + 1, 1 - parity, max_next, ms)
        max_after = stage_a(t + 2, parity)
        stage_c(t, parity, alphas)
        return ms, new_alphas, max_after

    groups = range(n_q * n_split)
    s_meta = [_dot(km_ref[0], qt_ref[0, g]) for g in groups]
    m_meta = [jnp.max(s, axis=0, keepdims=True) for s in s_meta]
    p_meta = [jnp.exp2(s - m).astype(BF16) for s, m in zip(s_meta, m_meta)]
    for g in groups:
        m0_ref[g // n_split, g % n_split] = m_meta[g]
        acc_ref[g // n_split, g % n_split] = _dot(vmt_ref[0], p_meta[g])

    max_next = stage_a(0, 0)
    ms, alphas = stage_b(0, 0, max_next, tuple(m_meta[:n_split]))
    carry = (ms, alphas, stage_a(1, 1))

    n_steps = n_tiles - 2
    n_loops = n_steps // unroll

    def body(i, carry):
        for u in range(unroll):
            carry = step(i * unroll + u, u % 2, carry)
        return carry

    carry = lax.fori_loop(0, n_loops, body, carry)
    for t in range(n_loops * unroll, n_steps):
        carry = step(t, t % 2, carry)
    ms, alphas, max_next = carry
    stage_c(n_tiles - 2, n_tiles % 2, alphas)
    ms, alphas = stage_b(n_tiles - 1, (n_tiles - 1) % 2, max_next, ms)
    stage_c(n_tiles - 1, (n_tiles - 1) % 2, alphas)

    for g in groups:
        acc = acc_ref[g // n_split, g % n_split]
        o = jnp.transpose(acc[:MLA_D_V] * (1.0 / acc[MLA_D_V:MLA_D_V + 1]))
        rows = slice(g * hw, (g + 1) * hw)
        o_ref[rows, :] = (o * z_ref[rows, :].astype(F32)).astype(BF16)


def _attention(p, meta, batch, seq, tq, tk):
    hw = p['q'].shape[3]
    n_split, n_q, n_kv = tq // hw, seq // tq, seq // tk
    assert tq % hw == 0 and tk % hw == 0 and n_q * n_kv >= 3
    return pl.pallas_call(
        functools.partial(_attn_kernel, tk=tk, n_q=n_q, n_kv=n_kv, unroll=ATTN_UNROLL),
        grid=(batch, MLA_HEADS),
        in_specs=[pl.BlockSpec((1, seq // hw, MLA_QK_PAD, hw), lambda b, h: (h, b, 0, 0)),
                  pl.BlockSpec((1, seq, MLA_QK_PAD), lambda b, h: (h, b, 0)),
                  pl.BlockSpec((1, seq // hw, MLA_V_EXT, hw), lambda b, h: (h, b, 0, 0)),
                  pl.BlockSpec((1, N_META, MLA_QK_PAD), lambda b, h: (h, 0, 0)),
                  pl.BlockSpec((1, MLA_V_EXT, N_META), lambda b, h: (h, 0, 0)),
                  pl.BlockSpec((seq, MLA_D_V), lambda b, h: (b, h))],
        out_specs=pl.BlockSpec((seq, MLA_D_V), lambda b, h: (b, h)),
        out_shape=jax.ShapeDtypeStruct((batch * seq, MLA_WIDTH), BF16),
        scratch_shapes=[pltpu.VMEM((n_q, n_split, MLA_V_EXT, hw), F32),
                        pltpu.VMEM((n_q, n_split, 1, hw), F32),
                        pltpu.VMEM((2, n_split, tk, hw), F32),
                        pltpu.VMEM((2, n_split, tk, hw), BF16)],
        compiler_params=pltpu.CompilerParams(dimension_semantics=("arbitrary", "arbitrary"),
                                             vmem_limit_bytes=VMEM_LIMIT),
        name="mla_attention",
    )(p['q'], p['k'], p['v'], meta['k'], meta['vt'], p['mz'])


def _out_kernel(x_ref, og_ref, gr_ref, om_ref, sa_ref, sb_ref, lng_ref, lnb_ref, woa_ref, wob_ref, wout_ref,
                png_ref, pnb_ref, y_ref):
    h = _layer_norm(x_ref[...], lng_ref[...], lnb_ref[...])
    a_in = (og_ref[...].astype(F32) * gr_ref[...].astype(F32)).astype(BF16)
    branch_a = _dot(a_in, woa_ref[...])
    branch_b = _dot(om_ref[...], wob_ref[...])
    mixed = sa_ref[...].astype(F32) * branch_a + sb_ref[...].astype(F32) * branch_b
    out = _dot(mixed.astype(BF16), wout_ref[...])
    y_ref[...] = _layer_norm(DEEPNORM_ALPHA * h + out, png_ref[...], pnb_ref[...])


def _out_stage(x2d, p, o_gla, o_mla, w, tm):
    n = x2d.shape[0]
    row = pl.BlockSpec((tm, D_MODEL), lambda i: (i, 0))
    consts = [w['emb_g'], w['emb_b'], w['w_o_gla'], w['w_o_mla'], w['w_out'], w['post_g'], w['post_b']]
    return pl.pallas_call(
        _out_kernel,
        grid=(n // tm,),
        in_specs=[row] * 6 + [_const_spec(c.shape) for c in consts],
        out_specs=row,
        out_shape=jax.ShapeDtypeStruct((n, D_MODEL), F32),
        compiler_params=pltpu.CompilerParams(dimension_semantics=("arbitrary",), vmem_limit_bytes=VMEM_LIMIT),
        name="merge_out",
    )(x2d, o_gla, p['gr'], o_mla, p['sa'], p['sb'], *consts)


def _rope_tables(start, length):
    inv_freq = 1.0 / (ROPE_THETA ** (jnp.arange(0, MLA_D_ROPE, 2, dtype=F32) / MLA_D_ROPE))
    ang = jnp.arange(start, start + length, dtype=F32)[:, None] * inv_freq[None, :]
    cos, sin = jnp.cos(ang), jnp.sin(ang)
    zero = jnp.zeros_like(cos)
    return (jnp.concatenate([cos, cos, zero, zero], axis=1),
            jnp.concatenate([zero, sin, zero, zero], axis=1),
            jnp.concatenate([-sin, zero, zero, zero], axis=1))


def _prepare_weights(emb_ln_g, emb_ln_b, w_in, b_merge, w_gla_gate_f, b_gla_gate_f, w_gla_gate_b, b_gla_gate_b,
                     gla_norm_g, w_o_gla, q_a_norm_g, w_q_b, kv_a_norm_g, w_kv_b, w_o_mla, w_out,
                     post_ln_g, post_ln_b):
    offs = [0]
    for s in IN_SPLITS:
        offs.append(offs[-1] + s)
    col = lambda i: w_in[0][:, offs[i]:offs[i + 1]]
    zcols = lambda n: jnp.zeros((D_MODEL, n), F32)
    w_sm = jnp.concatenate([col(4), col(5), zcols(LANES - 2 * GLA_GATE_RANK), col(6), col(7), col(8),
                            zcols(LANES - MLA_D_ROPE)], axis=1)
    w_gate = jnp.zeros((LANES, 2 * GLA_KEY_WIDTH), F32)
    w_gate = w_gate.at[:GLA_GATE_RANK, :GLA_KEY_WIDTH].set(w_gla_gate_f[0])
    w_gate = w_gate.at[GLA_GATE_RANK:2 * GLA_GATE_RANK, GLA_KEY_WIDTH:].set(w_gla_gate_b[0])
    wq = w_q_b[0].reshape(MLA_Q_RANK, MLA_HEADS, MLA_D_NOPE + MLA_D_ROPE)
    wq = jnp.pad(wq, ((0, 0), (0, 0), (0, MLA_QK_PAD - MLA_D_NOPE - MLA_D_ROPE)))
    wkv = w_kv_b[0].reshape(MLA_KV_RANK, MLA_HEADS, MLA_D_NOPE + MLA_D_V)
    wkv = jnp.concatenate([wkv[:, :, :MLA_D_NOPE].reshape(MLA_KV_RANK, -1),
                           wkv[:, :, MLA_D_NOPE:].reshape(MLA_KV_RANK, -1)], axis=1)
    r2 = lambda a: a.reshape(1, -1).astype(F32)
    return {
        'emb_g': r2(emb_ln_g), 'emb_b': r2(emb_ln_b),
        'w_qk': jnp.concatenate([col(0), col(1)], axis=1).astype(BF16),
        'w_v': col(2).astype(BF16), 'w_r': col(3).astype(BF16), 'w_z': col(9).astype(BF16),
        'w_a': col(10).astype(BF16), 'w_b': col(11).astype(BF16), 'w_sm': w_sm.astype(BF16),
        'b_ma': r2(b_merge[0][:D_MODEL]), 'b_mb': r2(b_merge[0][D_MODEL:]),
        'w_gate': w_gate.astype(BF16),
        'b_gate': r2(jnp.concatenate([b_gla_gate_f[0], b_gla_gate_b[0]])),
        'qn_g': r2(q_a_norm_g[0]), 'w_q': wq.reshape(MLA_Q_RANK, MLA_HEADS * MLA_QK_PAD).astype(BF16),
        'kvn_g': r2(kv_a_norm_g[0]), 'w_kv': wkv.astype(BF16),
        'gla_norm_g': r2(gla_norm_g[0]),
        'w_o_gla': w_o_gla[0].astype(BF16), 'w_o_mla': w_o_mla[0].astype(BF16), 'w_out': w_out[0].astype(BF16),
        'post_g': r2(post_ln_g[0]), 'post_b': r2(post_ln_b[0]),
    }


def _pick_tile(n, target):
    t = min(n, target)
    while n % t:
        t //= 2
    return t


def _encode(x, w, meta, s0, tm_in, tm_out, tq, tk):
    batch, seq, _ = x.shape
    x2d = x.reshape(batch * seq, D_MODEL)
    tm = _pick_tile(seq, tm_in)
    p = _inproj(x2d, tm, seq // tm, w, _rope_tables(N_META, seq), True)
    o_gla = _gla(p, s0, w['gla_norm_g'], batch, seq)
    o_mla = _attention(p, meta, batch, seq, _pick_tile(seq, tq), _pick_tile(seq, tk))
    y = _out_stage(x2d, p, o_gla, o_mla, w, _pick_tile(seq, tm_out))
    return y.reshape(batch, seq, D_MODEL)


def kernel(x_prompt, x_sample, meta_tokens, emb_ln_g, emb_ln_b, w_in, b_merge, w_gla_gate_f, b_gla_gate_f, w_gla_gate_b, b_gla_gate_b, gla_norm_g, w_o_gla, q_a_norm_g, w_q_b, kv_a_norm_g, w_kv_b, w_o_mla, w_out, post_ln_g, post_ln_b):
    w = _prepare_weights(emb_ln_g, emb_ln_b, w_in, b_merge, w_gla_gate_f, b_gla_gate_f, w_gla_gate_b,
                         b_gla_gate_b, gla_norm_g, w_o_gla, q_a_norm_g, w_q_b, kv_a_norm_g, w_kv_b, w_o_mla,
                         w_out, post_ln_g, post_ln_b)
    meta = _inproj(meta_tokens.astype(F32), N_META, 1, w, _rope_tables(0, N_META), False)
    vt = meta['v'].reshape(N_META, MLA_HEADS, MLA_D_V).transpose(1, 2, 0)
    ones_row = jnp.zeros((MLA_HEADS, MLA_V_EXT - MLA_D_V, N_META), BF16).at[:, 0, :].set(1.0)
    meta['vt'] = jnp.concatenate([vt, ones_row], axis=1)
    lead = ((GLA_CHUNK - N_META, 0), (0, 0))
    s0 = _gla_meta_state(jnp.pad(meta['gk'], lead), jnp.pad(meta['gv'], lead), jnp.pad(meta['gf'], lead))
    y_prompt = _encode(x_prompt, w, meta, s0, 256, 256, 512, 512)
    y_sample = _encode(x_sample, w, meta, s0, 256, 256, 512, 512)
    return (y_prompt, y_sample)
```

```python
import functools

import jax
import jax.numpy as jnp
from jax import lax
from jax.experimental import pallas as pl
from jax.experimental.pallas import tpu as pltpu

D_MODEL = 1024
N_META = 16
GLA_HEADS = 4
GLA_DK = 128
GLA_DV = 256
GLA_KEY_WIDTH = GLA_HEADS * GLA_DK
GLA_WIDTH = GLA_HEADS * GLA_DV
GLA_GATE_RANK = 16
GLA_TAU = 16.0
GLA_CHUNK = 64
MLA_HEADS = 8
MLA_D_NOPE = 128
MLA_D_ROPE = 64
MLA_D_V = 128
MLA_Q_RANK = 384
MLA_KV_RANK = 256
MLA_WIDTH = MLA_HEADS * MLA_D_V
ROPE_THETA = 10000.0
LN_EPS = 1e-5
RMS_EPS = 1e-6
DEPTH = 1
DEEPNORM_ALPHA = (2 * DEPTH) ** 0.25
IN_SPLITS = (GLA_KEY_WIDTH, GLA_KEY_WIDTH, GLA_WIDTH, GLA_WIDTH, GLA_GATE_RANK, GLA_GATE_RANK,
             MLA_Q_RANK, MLA_KV_RANK, MLA_D_ROPE, MLA_WIDTH, D_MODEL, D_MODEL)

LANES = 128
MLA_QK_PAD = 2 * LANES
BF16_SUBLANES = 16
MLA_V_EXT = MLA_D_V + BF16_SUBLANES
SMALL_COLS = LANES + MLA_Q_RANK + MLA_KV_RANK + LANES
VMEM_LIMIT = 56 * 1024 * 1024

LOG2_E = 1.4426950408889634
GLA_GROUP = 8
ATTN_UNROLL = 8

F32 = jnp.float32
BF16 = jnp.bfloat16
NT_DIMS = (((1,), (1,)), ((), ()))
TN_DIMS = (((0,), (0,)), ((), ()))


def _dot(a, b):
    return jnp.dot(a, b, preferred_element_type=F32)


def _sigmoid(x):
    return 1.0 / (1.0 + jnp.exp(-x))


def _layer_norm(x, g, b):
    mu = jnp.mean(x, axis=-1, keepdims=True)
    xc = x - mu
    var = jnp.mean(xc * xc, axis=-1, keepdims=True)
    return xc * lax.rsqrt(var + LN_EPS) * g + b


def _rms_norm(x, g):
    ms = jnp.mean(x * x, axis=-1, keepdims=True)
    return x * lax.rsqrt(ms + RMS_EPS) * g


def _rope(blk, c, s1, s2):
    return blk * c + pltpu.roll(blk, 32, 1) * s1 + pltpu.roll(blk, 96, 1) * s2


def _inproj_kernel(x_ref, lng_ref, lnb_ref, wqk_ref, wv_ref, wr_ref, wz_ref, wa_ref, wb_ref, wsm_ref,
                   bma_ref, bmb_ref, wgate_ref, bgate_ref, qng_ref, wq_ref, kvng_ref, wkv_ref,
                   rc_ref, rs1_ref, rs2_ref,
                   gq_ref, gk_ref, gv_ref, gr_ref, gf_ref, gb_ref, q_ref, k_ref, v_ref, mz_ref,
                   sa_ref, sb_ref, hb_ref, *, transposed):
    @pl.when(pl.program_id(0) == 0)
    def _():
        hb_ref[...] = jnp.zeros_like(hb_ref)

    def project(w_ref):
        return _dot(hb_ref[...], w_ref[...])

    p = project(wsm_ref)
    gx = _dot(p[:, :LANES].astype(BF16), wgate_ref[...]) + bgate_ref[...]
    cq = _rms_norm(p[:, LANES:LANES + MLA_Q_RANK], qng_ref[...]).astype(BF16)
    kv_lo = LANES + MLA_Q_RANK
    ckv = _rms_norm(p[:, kv_lo:kv_lo + MLA_KV_RANK], kvng_ref[...]).astype(BF16)
    kr_raw = p[:, kv_lo + MLA_KV_RANK:]

    pqk = project(wqk_ref)
    gq_ref[...] = (pqk[:, :GLA_KEY_WIDTH] * (GLA_DK ** -0.5)).astype(BF16)
    gk_ref[...] = pqk[:, GLA_KEY_WIDTH:].astype(BF16)

    ls = (jnp.minimum(gx, 0.0) - jnp.log1p(jnp.exp(-jnp.abs(gx)))) * (1.0 / GLA_TAU)
    gf_ref[...] = ls[:, :GLA_KEY_WIDTH]
    gb_ref[...] = ls[:, GLA_KEY_WIDTH:]

    gv_ref[...] = project(wv_ref).astype(BF16)
    r = project(wr_ref)
    gr_ref[...] = (r * _sigmoid(r)).astype(BF16)
    qf = _dot(cq, wq_ref[...])
    kvf = _dot(ckv, wkv_ref[...])
    z = project(wz_ref)
    mz_ref[...] = (z * _sigmoid(z)).astype(BF16)

    rc, rs1, rs2 = rc_ref[...], rs1_ref[...], rs2_ref[...]
    scale = (MLA_D_NOPE + MLA_D_ROPE) ** -0.5 * LOG2_E
    for hd in range(MLA_HEADS):
        lo = hd * MLA_QK_PAD
        q_nope = qf[:, lo:lo + LANES] * scale
        q_rope = _rope(qf[:, lo + LANES:lo + MLA_QK_PAD], rc, rs1, rs2) * scale
        if transposed:
            q_ref[hd, 0, :LANES, :] = q_nope.T.astype(BF16)
            q_ref[hd, 0, LANES:, :] = q_rope.T.astype(BF16)
        else:
            q_ref[hd, :, :LANES] = q_nope.astype(BF16)
            q_ref[hd, :, LANES:] = q_rope.astype(BF16)

    kr = _rope(kr_raw, rc, rs1, rs2).astype(BF16)
    for hd in range(MLA_HEADS):
        k_ref[hd, :, :LANES] = kvf[:, hd * LANES:(hd + 1) * LANES].astype(BF16)
        k_ref[hd, :, LANES:] = kr
    v_lo = MLA_HEADS * MLA_D_NOPE
    if transposed:
        tm = x_ref.shape[0]
        ones_row = (lax.broadcasted_iota(jnp.int32, (MLA_V_EXT - MLA_D_V, tm), 0) == 0).astype(BF16)
        for hd in range(MLA_HEADS):
            v_ref[hd, 0, :MLA_D_V, :] = kvf[:, v_lo + hd * MLA_D_V:v_lo + (hd + 1) * MLA_D_V].T.astype(BF16)
            v_ref[hd, 0, MLA_D_V:, :] = ones_row
    else:
        v_ref[...] = kvf[:, v_lo:].astype(BF16)

    sa_ref[...] = _sigmoid(project(wa_ref) + bma_ref[...]).astype(BF16)
    sb_ref[...] = _sigmoid(project(wb_ref) + bmb_ref[...]).astype(BF16)

    hb_ref[...] = _layer_norm(x_ref[...], lng_ref[...], lnb_ref[...]).astype(BF16)


def _const_spec(shape):
    nd = len(shape)
    return pl.BlockSpec(shape, lambda i: (0,) * nd, pipeline_mode=pl.Buffered(1))


def _inproj(x2d, tm, seq_tiles, w, rope_tabs, transposed):
    n = x2d.shape[0]
    n_tiles = n // tm
    prev = lambda i: jnp.maximum(i - 1, 0)
    row = lambda cols: pl.BlockSpec((tm, cols), lambda i: (prev(i), 0))
    tab = pl.BlockSpec((tm, LANES), lambda i: (prev(i) % seq_tiles, 0))
    consts = [w['emb_g'], w['emb_b'], w['w_qk'], w['w_v'], w['w_r'], w['w_z'], w['w_a'], w['w_b'], w['w_sm'],
              w['b_ma'], w['b_mb'], w['w_gate'], w['b_gate'], w['qn_g'], w['w_q'], w['kvn_g'], w['w_kv']]
    x_spec = pl.BlockSpec((tm, D_MODEL), lambda i: (jnp.minimum(i, n_tiles - 1), 0))
    in_specs = [x_spec] + [_const_spec(c.shape) for c in consts] + [tab, tab, tab]
    head_spec = pl.BlockSpec((MLA_HEADS, tm, MLA_QK_PAD), lambda i: (0, prev(i), 0))
    head_shape = jax.ShapeDtypeStruct((MLA_HEADS, n, MLA_QK_PAD), BF16)
    if transposed:
        q_spec = pl.BlockSpec((MLA_HEADS, 1, MLA_QK_PAD, tm), lambda i: (0, prev(i), 0, 0))
        q_shape = jax.ShapeDtypeStruct((MLA_HEADS, n_tiles, MLA_QK_PAD, tm), BF16)
        v_spec = pl.BlockSpec((MLA_HEADS, 1, MLA_V_EXT, tm), lambda i: (0, prev(i), 0, 0))
        v_shape = jax.ShapeDtypeStruct((MLA_HEADS, n_tiles, MLA_V_EXT, tm), BF16)
    else:
        q_spec, q_shape = head_spec, head_shape
        v_spec, v_shape = row(MLA_WIDTH), jax.ShapeDtypeStruct((n, MLA_WIDTH), BF16)
    out_shape = [
        jax.ShapeDtypeStruct((n, GLA_KEY_WIDTH), BF16), jax.ShapeDtypeStruct((n, GLA_KEY_WIDTH), BF16),
        jax.ShapeDtypeStruct((n, GLA_WIDTH), BF16), jax.ShapeDtypeStruct((n, GLA_WIDTH), BF16),
        jax.ShapeDtypeStruct((n, GLA_KEY_WIDTH), F32), jax.ShapeDtypeStruct((n, GLA_KEY_WIDTH), F32),
        q_shape, head_shape, v_shape, jax.ShapeDtypeStruct((n, MLA_WIDTH), BF16),
        jax.ShapeDtypeStruct((n, D_MODEL), BF16), jax.ShapeDtypeStruct((n, D_MODEL), BF16),
    ]
    out_specs = [row(GLA_KEY_WIDTH), row(GLA_KEY_WIDTH), row(GLA_WIDTH), row(GLA_WIDTH),
                 row(GLA_KEY_WIDTH), row(GLA_KEY_WIDTH), q_spec, head_spec,
                 v_spec, row(MLA_WIDTH), row(D_MODEL), row(D_MODEL)]
    outs = pl.pallas_call(
        functools.partial(_inproj_kernel, transposed=transposed),
        grid=(n_tiles + 1,),
        in_specs=in_specs,
        out_specs=out_specs,
        out_shape=out_shape,
        scratch_shapes=[pltpu.VMEM((tm, D_MODEL), BF16)],
        compiler_params=pltpu.CompilerParams(dimension_semantics=("arbitrary",), vmem_limit_bytes=VMEM_LIMIT),
        name="inproj",
    )(x2d, *consts, *rope_tabs)
    names = ('gq', 'gk', 'gv', 'gr', 'gf', 'gb', 'q', 'k', 'v', 'mz', 'sa', 'sb')
    return dict(zip(names, outs))


def _chunk_cumsum(tri, g):
    g_hi = g.astype(BF16)
    g_lo = (g - g_hi.astype(F32)).astype(BF16)
    r = _dot(tri, jnp.concatenate([g_hi, g_lo], axis=1))
    return r[:, :GLA_DK] + r[:, GLA_DK:]


def _decay_columns(b_last):
    d = jnp.transpose(jnp.broadcast_to(jnp.exp(b_last), (GLA_DK, GLA_DK)))
    return jnp.concatenate([d, d], axis=1)


def _chunk_consts():
    c = GLA_CHUNK
    r = lax.broadcasted_iota(jnp.int32, (c, c), 0)
    s = lax.broadcasted_iota(jnp.int32, (c, c), 1)
    rm = lax.broadcasted_iota(jnp.int32, (c, GLA_DK), 0)
    sm = lax.broadcasted_iota(jnp.int32, (c, GLA_DK), 1)
    return ((jnp.where(r >= s, 1.0, 0.0).astype(BF16), rm >= sm),
            (jnp.where(r <= s, 1.0, 0.0).astype(BF16), (rm < sm) & (sm < c)))


def _gla_superstep(q_ref, k_ref, v_ref, items, states):
    c = GLA_CHUNK
    zero_k = jnp.zeros((GLA_DK - c, GLA_DK), BF16)
    zero_v = jnp.zeros((GLA_DK - c, GLA_DV), BF16)
    bcs = [_chunk_cumsum(it['tri'], it['g_ref'][pl.ds(it['rows'], c), :]) for it in items]
    pre = []
    for it, bc in zip(items, bcs):
        q = q_ref[pl.ds(it['rows'], c), :].astype(F32)
        k = k_ref[pl.ds(it['rows'], c), :].astype(F32)
        b_mid = bc[it['mid']:it['mid'] + 1]
        b_last = bc[it['last']:it['last'] + 1]
        qa = q * jnp.exp(bc - b_mid)
        ka = k * jnp.exp(b_mid - bc)
        pre.append(dict(
            qa=qa.astype(BF16),
            ka=jnp.concatenate([ka.astype(BF16), zero_k], axis=0),
            q_in=(qa * jnp.exp(b_mid)).astype(BF16),
            k_dec=(ka * jnp.exp(b_last - b_mid)).astype(BF16),
            dcol=_decay_columns(b_last),
            v=v_ref[pl.ds(it['rows'], c), :]))
    a_list = [lax.dot_general(p['qa'], p['ka'], NT_DIMS, preferred_element_type=F32) for p in pre]
    kv_list = [lax.dot_general(p['k_dec'], p['v'], TN_DIMS, preferred_element_type=F32) for p in pre]
    states = list(states)
    for it, p, a, kv in zip(items, pre, a_list, kv_list):
        s = states[it['dirn']]
        lhs = jnp.concatenate([p['q_in'], jnp.where(it['mask'], a, 0.0).astype(BF16)], axis=1)
        rhs = jnp.concatenate([s.astype(BF16), p['v'], zero_v], axis=0)
        it['out_ref'][pl.ds(it['rows'], c), :] = _dot(lhs, rhs)
        states[it['dirn']] = s * p['dcol'] + kv
    return states


def _gla_kernel(q_ref, k_ref, v_ref, gf_ref, gb_ref, s0_ref, ng_ref, o_ref, of_ref, ob_ref, sf_ref, sb_ref, *,
                n_chunks, group):
    c = GLA_CHUNK
    (tri_f, mask_f), (tri_b, mask_b) = _chunk_consts()
    sf_ref[...] = s0_ref[0]
    sb_ref[...] = jnp.zeros_like(sb_ref)

    def body(i, carry):
        items = []
        for u in range(group):
            items.append(dict(rows=pl.multiple_of((i * group + u) * c, c), g_ref=gf_ref, tri=tri_f, mask=mask_f,
                              mid=c // 2, last=c - 1, dirn=0, out_ref=of_ref))
        for u in range(group):
            items.append(dict(rows=pl.multiple_of((n_chunks - 1 - i * group - u) * c, c), g_ref=gb_ref, tri=tri_b,
                              mask=mask_b, mid=c // 2 - 1, last=0, dirn=1, out_ref=ob_ref))
        s_f, s_b = _gla_superstep(q_ref, k_ref, v_ref, items, [sf_ref[...], sb_ref[...]])
        sf_ref[...] = s_f
        sb_ref[...] = s_b
        return carry

    lax.fori_loop(0, n_chunks // group, body, 0)

    blk = 4 * c

    def norm_body(i, carry):
        r0 = pl.multiple_of(i * blk, blk)
        o = of_ref[pl.ds(r0, blk), :] + ob_ref[pl.ds(r0, blk), :]
        o_ref[pl.ds(r0, blk), :] = _rms_norm(o, ng_ref[...]).astype(BF16)
        return carry

    lax.fori_loop(0, n_chunks * c // blk, norm_body, 0)


def _gla(p, s0, norm_g, batch, seq):
    n_chunks = seq // GLA_CHUNK
    kspec = pl.BlockSpec((seq, GLA_DK), lambda b, h: (b, h))
    vspec = pl.BlockSpec((seq, GLA_DV), lambda b, h: (b, h))
    return pl.pallas_call(
        functools.partial(_gla_kernel, n_chunks=n_chunks, group=_pick_tile(n_chunks, GLA_GROUP)),
        grid=(batch, GLA_HEADS),
        in_specs=[kspec, kspec, vspec, kspec, kspec,
                  pl.BlockSpec((1, GLA_DK, GLA_DV), lambda b, h: (h, 0, 0)),
                  pl.BlockSpec((1, GLA_DV), lambda b, h: (0, 0))],
        out_specs=vspec,
        out_shape=jax.ShapeDtypeStruct((batch * seq, GLA_WIDTH), BF16),
        scratch_shapes=[pltpu.VMEM((seq, GLA_DV), F32), pltpu.VMEM((seq, GLA_DV), F32),
                        pltpu.VMEM((GLA_DK, GLA_DV), F32), pltpu.VMEM((GLA_DK, GLA_DV), F32)],
        compiler_params=pltpu.CompilerParams(dimension_semantics=("arbitrary", "arbitrary"),
                                             vmem_limit_bytes=VMEM_LIMIT),
        name="gla",
    )(p['gq'], p['gk'], p['gv'], p['gf'], p['gb'], s0, norm_g)


def _gla_meta_state_kernel(k_ref, v_ref, g_ref, s_ref):
    (tri_f, _), _ = _chunk_consts()
    bc = _chunk_cumsum(tri_f, g_ref[...])
    k_dec = (k_ref[...].astype(F32) * jnp.exp(bc[GLA_CHUNK - 1:GLA_CHUNK] - bc)).astype(BF16)
    s_ref[0] = lax.dot_general(k_dec, v_ref[...], TN_DIMS, preferred_element_type=F32)


def _gla_meta_state(k_pad, v_pad, g_pad):
    return pl.pallas_call(
        _gla_meta_state_kernel,
        grid=(GLA_HEADS,),
        in_specs=[pl.BlockSpec((GLA_CHUNK, GLA_DK), lambda h: (0, h)),
                  pl.BlockSpec((GLA_CHUNK, GLA_DV), lambda h: (0, h)),
                  pl.BlockSpec((GLA_CHUNK, GLA_DK), lambda h: (0, h))],
        out_specs=pl.BlockSpec((1, GLA_DK, GLA_DV), lambda h: (h, 0, 0)),
        out_shape=jax.ShapeDtypeStruct((GLA_HEADS, GLA_DK, GLA_DV), F32),
        name="gla_meta_state",
    )(k_pad, v_pad, g_pad)


def _attn_kernel(qt_ref, k_ref, vt_ref, km_ref, vmt_ref, z_ref, o_ref, acc_ref, m0_ref, s_ref, p_ref, *,
                 tk, n_q, n_kv, unroll):
    n_split = acc_ref.shape[1]
    hw = acc_ref.shape[3]
    kc = tk // hw
    n_tiles = n_q * n_kv

    def stage_a(t, slot):
        qi, j = t // n_kv, t % n_kv
        k_j = k_ref[0, pl.ds(pl.multiple_of(j * tk, tk), tk), :]
        tile_max = []
        for a in range(n_split):
            s = _dot(k_j, qt_ref[0, qi * n_split + a])
            s_ref[slot, a] = s
            tile_max.append(jnp.max(s, axis=0, keepdims=True))
        return tuple(tile_max)

    def stage_b(t, slot, tile_max, ms):
        qi, j = t // n_kv, t % n_kv
        new_ms, alphas = [], []
        for a in range(n_split):
            m_old = jnp.where(j == 0, m0_ref[qi, a], ms[a])
            m_new = jnp.maximum(m_old, tile_max[a])
            alphas.append(jnp.exp2(m_old - m_new))
            p_ref[slot, a] = jnp.exp2(s_ref[slot, a] - m_new).astype(BF16)
            new_ms.append(m_new)
        return tuple(new_ms), tuple(alphas)

    def stage_c(t, slot, alphas):
        qi, j = t // n_kv, t % n_kv
        for a in range(n_split):
            pv = _dot(vt_ref[0, j * kc], p_ref[slot, a, :hw, :])
            for c in range(1, kc):
                pv += _dot(vt_ref[0, j * kc + c], p_ref[slot, a, c * hw:(c + 1) * hw, :])
            acc_ref[qi, a] = alphas[a] * acc_ref[qi, a] + pv

    def step(t, parity, carry):
        ms, alphas, max_next = carry
        ms, new_alphas = stage_b(t + 1, 1 - parity, max_next, ms)
        max_after = stage_a(t + 2, parity)
        stage_c(t, parity, alphas)
        return ms, new_alphas, max_after

    groups = range(n_q * n_split)
    s_meta = [_dot(km_ref[0], qt_ref[0, g]) for g in groups]
    m_meta = [jnp.max(s, axis=0, keepdims=True) for s in s_meta]
    p_meta = [jnp.exp2(s - m).astype(BF16) for s, m in zip(s_meta, m_meta)]
    for g in groups:
        m0_ref[g // n_split, g % n_split] = m_meta[g]
        acc_ref[g // n_split, g % n_split] = _dot(vmt_ref[0], p_meta[g])

    max_next = stage_a(0, 0)
    ms, alphas = stage_b(0, 0, max_next, tuple(m_meta[:n_split]))
    carry = (ms, alphas, stage_a(1, 1))

    n_steps = n_tiles - 2
    n_loops = n_steps // unroll

    def body(i, carry):
        for u in range(unroll):
            carry = step(i * unroll + u, u % 2, carry)
        return carry

    carry = lax.fori_loop(0, n_loops, body, carry)
    for t in range(n_loops * unroll, n_steps):
        carry = step(t, t % 2, carry)
    ms, alphas, max_next = carry
    stage_c(n_tiles - 2, n_tiles % 2, alphas)
    ms, alphas = stage_b(n_tiles - 1, (n_tiles - 1) % 2, max_next, ms)
    stage_c(n_tiles - 1, (n_tiles - 1) % 2, alphas)

    for g in groups:
        acc = acc_ref[g // n_split, g % n_split]
        o = jnp.transpose(acc[:MLA_D_V] * (1.0 / acc[MLA_D_V:MLA_D_V + 1]))
        rows = slice(g * hw, (g + 1) * hw)
        o_ref[rows, :] = (o * z_ref[rows, :].astype(F32)).astype(BF16)


def _attention(p, meta, batch, seq, tq, tk):
    hw = p['q'].shape[3]
    n_split, n_q, n_kv = tq // hw, seq // tq, seq // tk
    assert tq % hw == 0 and tk % hw == 0 and n_q * n_kv >= 3
    return pl.pallas_call(
        functools.partial(_attn_kernel, tk=tk, n_q=n_q, n_kv=n_kv, unroll=ATTN_UNROLL),
        grid=(batch, MLA_HEADS),
        in_specs=[pl.BlockSpec((1, seq // hw, MLA_QK_PAD, hw), lambda b, h: (h, b, 0, 0)),
                  pl.BlockSpec((1, seq, MLA_QK_PAD), lambda b, h: (h, b, 0)),
                  pl.BlockSpec((1, seq // hw, MLA_V_EXT, hw), lambda b, h: (h, b, 0, 0)),
                  pl.BlockSpec((1, N_META, MLA_QK_PAD), lambda b, h: (h, 0, 0)),
                  pl.BlockSpec((1, MLA_V_EXT, N_META), lambda b, h: (h, 0, 0)),
                  pl.BlockSpec((seq, MLA_D_V), lambda b, h: (b, h))],
        out_specs=pl.BlockSpec((seq, MLA_D_V), lambda b, h: (b, h)),
        out_shape=jax.ShapeDtypeStruct((batch * seq, MLA_WIDTH), BF16),
        scratch_shapes=[pltpu.VMEM((n_q, n_split, MLA_V_EXT, hw), F32),
                        pltpu.VMEM((n_q, n_split, 1, hw), F32),
                        pltpu.VMEM((2, n_split, tk, hw), F32),
                        pltpu.VMEM((2, n_split, tk, hw), BF16)],
        compiler_params=pltpu.CompilerParams(dimension_semantics=("arbitrary", "arbitrary"),
                                             vmem_limit_bytes=VMEM_LIMIT),
        name="mla_attention",
    )(p['q'], p['k'], p['v'], meta['k'], meta['vt'], p['mz'])


def _out_kernel(x_ref, og_ref, gr_ref, om_ref, sa_ref, sb_ref, lng_ref, lnb_ref, woa_ref, wob_ref, wout_ref,
                png_ref, pnb_ref, y_ref):
    h = _layer_norm(x_ref[...], lng_ref[...], lnb_ref[...])
    a_in = (og_ref[...].astype(F32) * gr_ref[...].astype(F32)).astype(BF16)
    branch_a = _dot(a_in, woa_ref[...])
    branch_b = _dot(om_ref[...], wob_ref[...])
    mixed = sa_ref[...].astype(F32) * branch_a + sb_ref[...].astype(F32) * branch_b
    out = _dot(mixed.astype(BF16), wout_ref[...])
    y_ref[...] = _layer_norm(DEEPNORM_ALPHA * h + out, png_ref[...], pnb_ref[...])


def _out_stage(x2d, p, o_gla, o_mla, w, tm):
    n = x2d.shape[0]
    row = pl.BlockSpec((tm, D_MODEL), lambda i: (i, 0))
    consts = [w['emb_g'], w['emb_b'], w['w_o_gla'], w['w_o_mla'], w['w_out'], w['post_g'], w['post_b']]
    return pl.pallas_call(
        _out_kernel,
        grid=(n // tm,),
        in_specs=[row] * 6 + [_const_spec(c.shape) for c in consts],
        out_specs=row,
        out_shape=jax.ShapeDtypeStruct((n, D_MODEL), F32),
        compiler_params=pltpu.CompilerParams(dimension_semantics=("arbitrary",), vmem_limit_bytes=VMEM_LIMIT),
        name="merge_out",
    )(x2d, o_gla, p['gr'], o_mla, p['sa'], p['sb'], *consts)


def _rope_tables(start, length):
    inv_freq = 1.0 / (ROPE_THETA ** (jnp.arange(0, MLA_D_ROPE, 2, dtype=F32) / MLA_D_ROPE))
    ang = jnp.arange(start, start + length, dtype=F32)[:, None] * inv_freq[None, :]
    cos, sin = jnp.cos(ang), jnp.sin(ang)
    zero = jnp.zeros_like(cos)
    return (jnp.concatenate([cos, cos, zero, zero], axis=1),
            jnp.concatenate([zero, sin, zero, zero], axis=1),
            jnp.concatenate([-sin, zero, zero, zero], axis=1))


def _prepare_weights(emb_ln_g, emb_ln_b, w_in, b_merge, w_gla_gate_f, b_gla_gate_f, w_gla_gate_b, b_gla_gate_b,
                     gla_norm_g, w_o_gla, q_a_norm_g, w_q_b, kv_a_norm_g, w_kv_b, w_o_mla, w_out,
                     post_ln_g, post_ln_b):
    offs = [0]
    for s in IN_SPLITS:
        offs.append(offs[-1] + s)
    col = lambda i: w_in[0][:, offs[i]:offs[i + 1]]
    zcols = lambda n: jnp.zeros((D_MODEL, n), F32)
    w_sm = jnp.concatenate([col(4), col(5), zcols(LANES - 2 * GLA_GATE_RANK), col(6), col(7), col(8),
                            zcols(LANES - MLA_D_ROPE)], axis=1)
    w_gate = jnp.zeros((LANES, 2 * GLA_KEY_WIDTH), F32)
    w_gate = w_gate.at[:GLA_GATE_RANK, :GLA_KEY_WIDTH].set(w_gla_gate_f[0])
    w_gate = w_gate.at[GLA_GATE_RANK:2 * GLA_GATE_RANK, GLA_KEY_WIDTH:].set(w_gla_gate_b[0])
    wq = w_q_b[0].reshape(MLA_Q_RANK, MLA_HEADS, MLA_D_NOPE + MLA_D_ROPE)
    wq = jnp.pad(wq, ((0, 0), (0, 0), (0, MLA_QK_PAD - MLA_D_NOPE - MLA_D_ROPE)))
    wkv = w_kv_b[0].reshape(MLA_KV_RANK, MLA_HEADS, MLA_D_NOPE + MLA_D_V)
    wkv = jnp.concatenate([wkv[:, :, :MLA_D_NOPE].reshape(MLA_KV_RANK, -1),
                           wkv[:, :, MLA_D_NOPE:].reshape(MLA_KV_RANK, -1)], axis=1)
    r2 = lambda a: a.reshape(1, -1).astype(F32)
    return {
        'emb_g': r2(emb_ln_g), 'emb_b': r2(emb_ln_b),
        'w_qk': jnp.concatenate([col(0), col(1)], axis=1).astype(BF16),
        'w_v': col(2).astype(BF16), 'w_r': col(3).astype(BF16), 'w_z': col(9).astype(BF16),
        'w_a': col(10).astype(BF16), 'w_b': col(11).astype(BF16), 'w_sm': w_sm.astype(BF16),
        'b_ma': r2(b_merge[0][:D_MODEL]), 'b_mb': r2(b_merge[0][D_MODEL:]),
        'w_gate': w_gate.astype(BF16),
        'b_gate': r2(jnp.concatenate([b_gla_gate_f[0], b_gla_gate_b[0]])),
        'qn_g': r2(q_a_norm_g[0]), 'w_q': wq.reshape(MLA_Q_RANK, MLA_HEADS * MLA_QK_PAD).astype(BF16),
        'kvn_g': r2(kv_a_norm_g[0]), 'w_kv': wkv.astype(BF16),
        'gla_norm_g': r2(gla_norm_g[0]),
        'w_o_gla': w_o_gla[0].astype(BF16), 'w_o_mla': w_o_mla[0].astype(BF16), 'w_out': w_out[0].astype(BF16),
        'post_g': r2(post_ln_g[0]), 'post_b': r2(post_ln_b[0]),
    }


def _pick_tile(n, target):
    t = min(n, target)
    while n % t:
        t //= 2
    return t


def _encode(x, w, meta, s0, tm_in, tm_out, tq, tk):
    batch, seq, _ = x.shape
    x2d = x.reshape(batch * seq, D_MODEL)
    tm = _pick_tile(seq, tm_in)
    p = _inproj(x2d, tm, seq // tm, w, _rope_tables(N_META, seq), True)
    o_gla = _gla(p, s0, w['gla_norm_g'], batch, seq)
    o_mla = _attention(p, meta, batch, seq, _pick_tile(seq, tq), _pick_tile(seq, tk))
    y = _out_stage(x2d, p, o_gla, o_mla, w, _pick_tile(seq, tm_out))
    return y.reshape(batch, seq, D_MODEL)


def kernel(x_prompt, x_sample, meta_tokens, emb_ln_g, emb_ln_b, w_in, b_merge, w_gla_gate_f, b_gla_gate_f, w_gla_gate_b, b_gla_gate_b, gla_norm_g, w_o_gla, q_a_norm_g, w_q_b, kv_a_norm_g, w_kv_b, w_o_mla, w_out, post_ln_g, post_ln_b):
    w = _prepare_weights(emb_ln_g, emb_ln_b, w_in, b_merge, w_gla_gate_f, b_gla_gate_f, w_gla_gate_b,
                         b_gla_gate_b, gla_norm_g, w_o_gla, q_a_norm_g, w_q_b, kv_a_norm_g, w_kv_b, w_o_mla,
                         w_out, post_ln_g, post_ln_b)
    meta = _inproj(meta_tokens.astype(F32), N_META, 1, w, _rope_tables(0, N_META), False)
    vt = meta['v'].reshape(N_META, MLA_HEADS, MLA_D_V).transpose(1, 2, 0)
    ones_row = jnp.zeros((MLA_HEADS, MLA_V_EXT - MLA_D_V, N_META), BF16).at[:, 0, :].set(1.0)
    meta['vt'] = jnp.concatenate([vt, ones_row], axis=1)
    lead = ((GLA_CHUNK - N_META, 0), (0, 0))
    s0 = _gla_meta_state(jnp.pad(meta['gk'], lead), jnp.pad(meta['gv'], lead), jnp.pad(meta['gf'], lead))
    y_prompt = _encode(x_prompt, w, meta, s0, 256, 256, 512, 512)
    y_sample = _encode(x_sample, w, meta, s0, 256, 256, 512, 512)
    return (y_prompt, y_sample)
```

```python
import functools

import jax
import jax.numpy as jnp
from jax import lax
from jax.experimental import pallas as pl
from jax.experimental.pallas import tpu as pltpu

D_MODEL = 1024
N_META = 16
GLA_HEADS = 4
GLA_DK = 128
GLA_DV = 256
GLA_KEY_WIDTH = GLA_HEADS * GLA_DK
GLA_WIDTH = GLA_HEADS * GLA_DV
GLA_GATE_RANK = 16
GLA_TAU = 16.0
GLA_CHUNK = 64
MLA_HEADS = 8
MLA_D_NOPE = 128
MLA_D_ROPE = 64
MLA_D_V = 128
MLA_Q_RANK = 384
MLA_KV_RANK = 256
MLA_WIDTH = MLA_HEADS * MLA_D_V
ROPE_THETA = 10000.0
LN_EPS = 1e-5
RMS_EPS = 1e-6
DEPTH = 1
DEEPNORM_ALPHA = (2 * DEPTH) ** 0.25
IN_SPLITS = (GLA_KEY_WIDTH, GLA_KEY_WIDTH, GLA_WIDTH, GLA_WIDTH, GLA_GATE_RANK, GLA_GATE_RANK,
             MLA_Q_RANK, MLA_KV_RANK, MLA_D_ROPE, MLA_WIDTH, D_MODEL, D_MODEL)

LANES = 128
MXU_COLS = 256
MLA_QK_PAD = 2 * LANES
BF16_SUBLANES = 16
MLA_V_EXT = MLA_D_V + BF16_SUBLANES
SMALL_COLS = LANES + MLA_Q_RANK + MLA_KV_RANK + LANES
VMEM_LIMIT = 56 * 1024 * 1024

LOG2_E = 1.4426950408889634
GLA_GROUP = 8
ATTN_SLOTS = 2
ATTN_UNROLL = 8

F32 = jnp.float32
BF16 = jnp.bfloat16
NT_DIMS = (((1,), (1,)), ((), ()))
TN_DIMS = (((0,), (0,)), ((), ()))


def _dot(a, b):
    return jnp.dot(a, b, preferred_element_type=F32)


def _sigmoid(x):
    return 1.0 / (1.0 + jnp.exp(-x))


def _layer_norm(x, g, b):
    mu = jnp.mean(x, axis=-1, keepdims=True)
    xc = x - mu
    var = jnp.mean(xc * xc, axis=-1, keepdims=True)
    return xc * lax.rsqrt(var + LN_EPS) * g + b


def _rms_norm(x, g):
    ms = jnp.mean(x * x, axis=-1, keepdims=True)
    return x * lax.rsqrt(ms + RMS_EPS) * g


def _rope(blk, c, s1, s2):
    return blk * c + pltpu.roll(blk, 32, 1) * s1 + pltpu.roll(blk, 96, 1) * s2


def _inproj_kernel(x_ref, lng_ref, lnb_ref, wqk_ref, wv_ref, wr_ref, wz_ref, wa_ref, wb_ref, wsm_ref,
                   bma_ref, bmb_ref, wgate_ref, bgate_ref, qng_ref, wq_ref, kvng_ref, wkv_ref,
                   rc_ref, rs1_ref, rs2_ref,
                   gq_ref, gk_ref, gv_ref, gr_ref, gf_ref, gb_ref, q_ref, k_ref, v_ref, mz_ref,
                   sa_ref, sb_ref, hb_ref, *, transposed):
    @pl.when(pl.program_id(0) == 0)
    def _():
        hb_ref[...] = jnp.zeros_like(hb_ref)

    def tiles(lhs, w_ref, emit):
        for j in range(w_ref.shape[1] // MXU_COLS):
            cols = slice(j * MXU_COLS, (j + 1) * MXU_COLS)
            emit(j, cols, _dot(lhs, w_ref[:, cols]))

    def project(w_ref, emit):
        tiles(hb_ref[...], w_ref, emit)

    def split_store(lo_ref, hi_ref, f_lo, f_hi):
        half = lo_ref.shape[1] // MXU_COLS

        def emit(j, cols, t):
            if j < half:
                lo_ref[:, cols] = f_lo(t, cols)
            else:
                hi_ref[:, (j - half) * MXU_COLS:(j - half + 1) * MXU_COLS] = f_hi(t, cols)
        return emit

    def store(ref, f):
        def emit(j, cols, t):
            ref[:, cols] = f(t, cols)
        return emit

    p = _dot(hb_ref[...], wsm_ref[...])
    gl = p[:, :LANES].astype(BF16)
    cq = _rms_norm(p[:, LANES:LANES + MLA_Q_RANK], qng_ref[...]).astype(BF16)
    kv_lo = LANES + MLA_Q_RANK
    ckv = _rms_norm(p[:, kv_lo:kv_lo + MLA_KV_RANK], kvng_ref[...]).astype(BF16)
    kr_raw = p[:, kv_lo + MLA_KV_RANK:]

    as_bf16 = lambda t, cols: t.astype(BF16)
    project(wqk_ref, split_store(gq_ref, gk_ref, lambda t, cols: (t * (GLA_DK ** -0.5)).astype(BF16), as_bf16))

    def log_decay(t, cols):
        gx = t + bgate_ref[:, cols]
        return (jnp.minimum(gx, 0.0) - jnp.log1p(jnp.exp(-jnp.abs(gx)))) * (1.0 / GLA_TAU)
    tiles(gl, wgate_ref, split_store(gf_ref, gb_ref, log_decay, log_decay))

    silu = lambda t, cols: (t * _sigmoid(t)).astype(BF16)
    project(wv_ref, store(gv_ref, as_bf16))
    project(wr_ref, store(gr_ref, silu))

    rc, rs1, rs2 = rc_ref[...], rs1_ref[...], rs2_ref[...]
    scale = (MLA_D_NOPE + MLA_D_ROPE) ** -0.5 * LOG2_E

    def emit_q(hd, cols, t):
        q_nope = t[:, :LANES] * scale
        q_rope = _rope(t[:, LANES:], rc, rs1, rs2) * scale
        if transposed:
            q_ref[hd, 0, :LANES, :] = q_nope.T.astype(BF16)
            q_ref[hd, 0, LANES:, :] = q_rope.T.astype(BF16)
        else:
            q_ref[hd, :, :LANES] = q_nope.astype(BF16)
            q_ref[hd, :, LANES:] = q_rope.astype(BF16)
    tiles(cq, wq_ref, emit_q)

    kr = _rope(kr_raw, rc, rs1, rs2).astype(BF16)
    tm = x_ref.shape[0]
    ones_row = (lax.broadcasted_iota(jnp.int32, (MLA_V_EXT - MLA_D_V, tm), 0) == 0).astype(BF16)
    heads_per_tile = MXU_COLS // LANES

    def emit_kv(j, cols, t):
        for u in range(heads_per_tile):
            part = t[:, u * LANES:(u + 1) * LANES]
            hd = (j * heads_per_tile + u) % MLA_HEADS
            if j * heads_per_tile + u < MLA_HEADS:
                k_ref[hd, :, :LANES] = part.astype(BF16)
                k_ref[hd, :, LANES:] = kr
            elif transposed:
                v_ref[hd, 0, :MLA_D_V, :] = part.T.astype(BF16)
                v_ref[hd, 0, MLA_D_V:, :] = ones_row
            else:
                v_ref[:, hd * MLA_D_V:(hd + 1) * MLA_D_V] = part.astype(BF16)
    tiles(ckv, wkv_ref, emit_kv)

    project(wz_ref, store(mz_ref, silu))
    project(wa_ref, store(sa_ref, lambda t, cols: _sigmoid(t + bma_ref[:, cols]).astype(BF16)))
    project(wb_ref, store(sb_ref, lambda t, cols: _sigmoid(t + bmb_ref[:, cols]).astype(BF16)))

    hb_ref[...] = _layer_norm(x_ref[...], lng_ref[...], lnb_ref[...]).astype(BF16)


def _const_spec(shape):
    nd = len(shape)
    return pl.BlockSpec(shape, lambda i: (0,) * nd, pipeline_mode=pl.Buffered(1))


def _inproj(x2d, tm, seq_tiles, w, rope_tabs, transposed):
    n = x2d.shape[0]
    n_tiles = n // tm
    prev = lambda i: jnp.maximum(i - 1, 0)
    row = lambda cols: pl.BlockSpec((tm, cols), lambda i: (prev(i), 0))
    tab = pl.BlockSpec((tm, LANES), lambda i: (prev(i) % seq_tiles, 0))
    consts = [w['emb_g'], w['emb_b'], w['w_qk'], w['w_v'], w['w_r'], w['w_z'], w['w_a'], w['w_b'], w['w_sm'],
              w['b_ma'], w['b_mb'], w['w_gate'], w['b_gate'], w['qn_g'], w['w_q'], w['kvn_g'], w['w_kv']]
    x_spec = pl.BlockSpec((tm, D_MODEL), lambda i: (jnp.minimum(i, n_tiles - 1), 0))
    in_specs = [x_spec] + [_const_spec(c.shape) for c in consts] + [tab, tab, tab]
    head_spec = pl.BlockSpec((MLA_HEADS, tm, MLA_QK_PAD), lambda i: (0, prev(i), 0))
    head_shape = jax.ShapeDtypeStruct((MLA_HEADS, n, MLA_QK_PAD), BF16)
    if transposed:
        q_spec = pl.BlockSpec((MLA_HEADS, 1, MLA_QK_PAD, tm), lambda i: (0, prev(i), 0, 0))
        q_shape = jax.ShapeDtypeStruct((MLA_HEADS, n_tiles, MLA_QK_PAD, tm), BF16)
        v_spec = pl.BlockSpec((MLA_HEADS, 1, MLA_V_EXT, tm), lambda i: (0, prev(i), 0, 0))
        v_shape = jax.ShapeDtypeStruct((MLA_HEADS, n_tiles, MLA_V_EXT, tm), BF16)
    else:
        q_spec, q_shape = head_spec, head_shape
        v_spec, v_shape = row(MLA_WIDTH), jax.ShapeDtypeStruct((n, MLA_WIDTH), BF16)
    out_shape = [
        jax.ShapeDtypeStruct((n, GLA_KEY_WIDTH), BF16), jax.ShapeDtypeStruct((n, GLA_KEY_WIDTH), BF16),
        jax.ShapeDtypeStruct((n, GLA_WIDTH), BF16), jax.ShapeDtypeStruct((n, GLA_WIDTH), BF16),
        jax.ShapeDtypeStruct((n, GLA_KEY_WIDTH), F32), jax.ShapeDtypeStruct((n, GLA_KEY_WIDTH), F32),
        q_shape, head_shape, v_shape, jax.ShapeDtypeStruct((n, MLA_WIDTH), BF16),
        jax.ShapeDtypeStruct((n, D_MODEL), BF16), jax.ShapeDtypeStruct((n, D_MODEL), BF16),
    ]
    out_specs = [row(GLA_KEY_WIDTH), row(GLA_KEY_WIDTH), row(GLA_WIDTH), row(GLA_WIDTH),
                 row(GLA_KEY_WIDTH), row(GLA_KEY_WIDTH), q_spec, head_spec,
                 v_spec, row(MLA_WIDTH), row(D_MODEL), row(D_MODEL)]
    outs = pl.pallas_call(
        functools.partial(_inproj_kernel, transposed=transposed),
        grid=(n_tiles + 1,),
        in_specs=in_specs,
        out_specs=out_specs,
        out_shape=out_shape,
        scratch_shapes=[pltpu.VMEM((tm, D_MODEL), BF16)],
        compiler_params=pltpu.CompilerParams(dimension_semantics=("arbitrary",), vmem_limit_bytes=VMEM_LIMIT),
        name="inproj",
    )(x2d, *consts, *rope_tabs)
    names = ('gq', 'gk', 'gv', 'gr', 'gf', 'gb', 'q', 'k', 'v', 'mz', 'sa', 'sb')
    return dict(zip(names, outs))


def _chunk_cumsum(tri, g):
    g_hi = g.astype(BF16)
    g_lo = (g - g_hi.astype(F32)).astype(BF16)
    r = _dot(tri, jnp.concatenate([g_hi, g_lo], axis=1))
    return r[:, :GLA_DK] + r[:, GLA_DK:]


def _chunk_consts():
    c = GLA_CHUNK
    r = lax.broadcasted_iota(jnp.int32, (c, c), 0)
    s = lax.broadcasted_iota(jnp.int32, (c, c), 1)
    rm = lax.broadcasted_iota(jnp.int32, (c, GLA_DK), 0)
    sm = lax.broadcasted_iota(jnp.int32, (c, GLA_DK), 1)
    return ((jnp.where(r >= s, 1.0, 0.0).astype(BF16), rm >= sm),
            (jnp.where(r <= s, 1.0, 0.0).astype(BF16), (rm < sm) & (sm < c)))


def _gla_prepare(q_ref, k_ref, items):
    c = GLA_CHUNK
    bcs = [_chunk_cumsum(it['tri'], it['g_ref'][pl.ds(it['rows'], c), :]) for it in items]
    pre = []
    for it, bc in zip(items, bcs):
        q = q_ref[pl.ds(it['rows'], c), :].astype(F32)
        k = k_ref[pl.ds(it['rows'], c), :].astype(F32)
        b_mid = bc[it['mid']:it['mid'] + 1]
        b_last = bc[it['last']:it['last'] + 1]
        qa = q * jnp.exp(bc - b_mid)
        ka = k * jnp.exp(b_mid - bc)
        pre.append(dict(
            qa=qa.astype(BF16), ka=ka.astype(BF16),
            q_in=(qa * jnp.exp(b_mid)).astype(BF16),
            k_dec=(ka * jnp.exp(b_last - b_mid)).astype(BF16),
            dcol=jnp.transpose(jnp.broadcast_to(jnp.exp(b_last), (GLA_DK, GLA_DK)))))
    return pre


def _gla_apply(v_ref, items, pre, states):
    c = GLA_CHUNK
    zero_k = jnp.zeros((GLA_DK - c, GLA_DK), BF16)
    zero_v = jnp.zeros((GLA_DK - c, GLA_DV), BF16)
    vs = [v_ref[pl.ds(it['rows'], c), :] for it in items]
    a_list = [lax.dot_general(p['qa'], jnp.concatenate([p['ka'], zero_k], axis=0), NT_DIMS,
                              preferred_element_type=F32) for p in pre]
    kv_list = [lax.dot_general(p['k_dec'], v, TN_DIMS, preferred_element_type=F32) for p, v in zip(pre, vs)]
    states = list(states)
    for it, p, v, a, kv in zip(items, pre, vs, a_list, kv_list):
        s = states[it['dirn']]
        lhs = jnp.concatenate([p['q_in'], jnp.where(it['mask'], a, 0.0).astype(BF16)], axis=1)
        rhs = jnp.concatenate([s.astype(BF16), v, zero_v], axis=0)
        it['out_ref'][pl.ds(it['rows'], c), :] = _dot(lhs, rhs).astype(BF16)
        states[it['dirn']] = s * jnp.concatenate([p['dcol'], p['dcol']], axis=1) + kv
    return states


def _gla_kernel(q_ref, k_ref, v_ref, gf_ref, gb_ref, s0_ref, of_ref, ob_ref, sf_ref, sb_ref, *, n_chunks, group):
    c = GLA_CHUNK
    (tri_f, mask_f), (tri_b, mask_b) = _chunk_consts()
    sf_ref[...] = s0_ref[0]
    sb_ref[...] = jnp.zeros_like(sb_ref)

    def body(i, carry):
        items = []
        for u in range(group):
            items.append(dict(rows=pl.multiple_of((i * group + u) * c, c), g_ref=gf_ref, tri=tri_f, mask=mask_f,
                              mid=c // 2, last=c - 1, dirn=0, out_ref=of_ref))
        for u in range(group):
            items.append(dict(rows=pl.multiple_of((n_chunks - 1 - i * group - u) * c, c), g_ref=gb_ref, tri=tri_b,
                              mask=mask_b, mid=c // 2 - 1, last=0, dirn=1, out_ref=ob_ref))
        pre = _gla_prepare(q_ref, k_ref, items)
        s_f, s_b = _gla_apply(v_ref, items, pre, [sf_ref[...], sb_ref[...]])
        sf_ref[...] = s_f
        sb_ref[...] = s_b
        return carry

    lax.fori_loop(0, n_chunks // group, body, 0)


def _gla(p, s0, batch, seq):
    n_chunks = seq // GLA_CHUNK
    group = _pick_tile(n_chunks, GLA_GROUP)
    kspec = pl.BlockSpec((seq, GLA_DK), lambda b, h: (b, h))
    vspec = pl.BlockSpec((seq, GLA_DV), lambda b, h: (b, h))
    out = jax.ShapeDtypeStruct((batch * seq, GLA_WIDTH), BF16)
    return pl.pallas_call(
        functools.partial(_gla_kernel, n_chunks=n_chunks, group=group),
        grid=(batch, GLA_HEADS),
        in_specs=[kspec, kspec, vspec, kspec, kspec,
                  pl.BlockSpec((1, GLA_DK, GLA_DV), lambda b, h: (h, 0, 0))],
        out_specs=[vspec, vspec],
        out_shape=[out, out],
        scratch_shapes=[pltpu.VMEM((GLA_DK, GLA_DV), F32), pltpu.VMEM((GLA_DK, GLA_DV), F32)],
        compiler_params=pltpu.CompilerParams(dimension_semantics=("arbitrary", "arbitrary"),
                                             vmem_limit_bytes=VMEM_LIMIT),
        name="gla",
    )(p['gq'], p['gk'], p['gv'], p['gf'], p['gb'], s0)


def _gla_meta_state_kernel(k_ref, v_ref, g_ref, s_ref):
    (tri_f, _), _ = _chunk_consts()
    bc = _chunk_cumsum(tri_f, g_ref[...])
    k_dec = (k_ref[...].astype(F32) * jnp.exp(bc[GLA_CHUNK - 1:GLA_CHUNK] - bc)).astype(BF16)
    s_ref[0] = lax.dot_general(k_dec, v_ref[...], TN_DIMS, preferred_element_type=F32)


def _gla_meta_state(k_pad, v_pad, g_pad):
    return pl.pallas_call(
        _gla_meta_state_kernel,
        grid=(GLA_HEADS,),
        in_specs=[pl.BlockSpec((GLA_CHUNK, GLA_DK), lambda h: (0, h)),
                  pl.BlockSpec((GLA_CHUNK, GLA_DV), lambda h: (0, h)),
                  pl.BlockSpec((GLA_CHUNK, GLA_DK), lambda h: (0, h))],
        out_specs=pl.BlockSpec((1, GLA_DK, GLA_DV), lambda h: (h, 0, 0)),
        out_shape=jax.ShapeDtypeStruct((GLA_HEADS, GLA_DK, GLA_DV), F32),
        name="gla_meta_state",
    )(k_pad, v_pad, g_pad)


def _attn_kernel(qt_ref, k_ref, vt_ref, km_ref, vmt_ref, z_ref, o_ref, acc_ref, m0_ref, s_ref, p_ref, *,
                 tk, n_q, n_kv, unroll):
    n_split = acc_ref.shape[1]
    hw = acc_ref.shape[3]
    kc = tk // hw
    n_tiles = n_q * n_kv

    def stage_a(t, slot):
        qi, j = t // n_kv, t % n_kv
        tile_max = []
        for a in range(n_split):
            k_j = k_ref[0, pl.ds(pl.multiple_of(j * tk, tk), tk), :]
            s = _dot(k_j, qt_ref[0, qi * n_split + a])
            s_ref[slot, a] = s
            tile_max.append(jnp.max(s, axis=0, keepdims=True))
        return tuple(tile_max)

    def stage_b(t, slot, tile_max, ms):
        qi, j = t // n_kv, t % n_kv
        new_ms, alphas = [], []
        for a in range(n_split):
            m_old = jnp.where(j == 0, m0_ref[qi, a], ms[a])
            m_new = jnp.maximum(m_old, tile_max[a])
            alphas.append(jnp.exp2(m_old - m_new))
            p_ref[slot, a] = jnp.exp2(s_ref[slot, a] - m_new).astype(BF16)
            new_ms.append(m_new)
        return tuple(new_ms), tuple(alphas)

    def stage_c(t, slot, alphas):
        qi, j = t // n_kv, t % n_kv
        for a in range(n_split):
            pv = _dot(vt_ref[0, j * kc], p_ref[slot, a, :hw, :])
            for c in range(1, kc):
                pv += _dot(vt_ref[0, j * kc + c], p_ref[slot, a, c * hw:(c + 1) * hw, :])
            acc_ref[qi, a] = alphas[a] * acc_ref[qi, a] + pv

    def step(t, slot, carry):
        ms, alphas, max_next = carry
        ms, new_alphas = stage_b(t + 1, (slot + 1) % ATTN_SLOTS, max_next, ms)
        max_after = stage_a(t + 2, (slot + 2) % ATTN_SLOTS)
        stage_c(t, slot, alphas)
        return ms, new_alphas, max_after

    groups = range(n_q * n_split)
    s_meta = [_dot(km_ref[0], qt_ref[0, g]) for g in groups]
    m_meta = [jnp.max(s, axis=0, keepdims=True) for s in s_meta]
    p_meta = [jnp.exp2(s - m).astype(BF16) for s, m in zip(s_meta, m_meta)]
    for g in groups:
        m0_ref[g // n_split, g % n_split] = m_meta[g]
        acc_ref[g // n_split, g % n_split] = _dot(vmt_ref[0], p_meta[g])

    max_next = stage_a(0, 0)
    ms, alphas = stage_b(0, 0, max_next, tuple(m_meta[:n_split]))
    carry = (ms, alphas, stage_a(1, 1))

    n_steps = n_tiles - 2
    n_loops = n_steps // unroll

    def body(i, carry):
        for u in range(unroll):
            carry = step(i * unroll + u, u % ATTN_SLOTS, carry)
        return carry

    carry = lax.fori_loop(0, n_loops, body, carry)
    for t in range(n_loops * unroll, n_steps):
        carry = step(t, t % ATTN_SLOTS, carry)
    ms, alphas, max_next = carry
    stage_c(n_tiles - 2, (n_tiles - 2) % ATTN_SLOTS, alphas)
    ms, alphas = stage_b(n_tiles - 1, (n_tiles - 1) % ATTN_SLOTS, max_next, ms)
    stage_c(n_tiles - 1, (n_tiles - 1) % ATTN_SLOTS, alphas)

    for g in groups:
        acc = acc_ref[g // n_split, g % n_split]
        o = jnp.transpose(acc[:MLA_D_V] * (1.0 / acc[MLA_D_V:MLA_D_V + 1]))
        rows = slice(g * hw, (g + 1) * hw)
        o_ref[rows, :] = (o * z_ref[rows, :].astype(F32)).astype(BF16)


def _attention(p, meta, batch, seq, tq, tk):
    hw = p['q'].shape[3]
    n_split, n_q, n_kv = tq // hw, seq // tq, seq // tk
    assert tq % hw == 0 and tk % hw == 0 and n_q * n_kv >= 3
    return pl.pallas_call(
        functools.partial(_attn_kernel, tk=tk, n_q=n_q, n_kv=n_kv, unroll=ATTN_UNROLL),
        grid=(batch, MLA_HEADS),
        in_specs=[pl.BlockSpec((1, seq // hw, MLA_QK_PAD, hw), lambda b, h: (h, b, 0, 0)),
                  pl.BlockSpec((1, seq, MLA_QK_PAD), lambda b, h: (h, b, 0)),
                  pl.BlockSpec((1, seq // hw, MLA_V_EXT, hw), lambda b, h: (h, b, 0, 0)),
                  pl.BlockSpec((1, N_META, MLA_QK_PAD), lambda b, h: (h, 0, 0)),
                  pl.BlockSpec((1, MLA_V_EXT, N_META), lambda b, h: (h, 0, 0)),
                  pl.BlockSpec((seq, MLA_D_V), lambda b, h: (b, h))],
        out_specs=pl.BlockSpec((seq, MLA_D_V), lambda b, h: (b, h)),
        out_shape=jax.ShapeDtypeStruct((batch * seq, MLA_WIDTH), BF16),
        scratch_shapes=[pltpu.VMEM((n_q, n_split, MLA_V_EXT, hw), F32),
                        pltpu.VMEM((n_q, n_split, 1, hw), F32),
                        pltpu.VMEM((ATTN_SLOTS, n_split, tk, hw), F32),
                        pltpu.VMEM((ATTN_SLOTS, n_split, tk, hw), BF16)],
        compiler_params=pltpu.CompilerParams(dimension_semantics=("arbitrary", "arbitrary"),
                                             vmem_limit_bytes=VMEM_LIMIT),
        name="mla_attention",
    )(p['q'], p['k'], p['v'], meta['k'], meta['vt'], p['mz'])


def _out_kernel(x_ref, of_ref, ob_ref, gr_ref, om_ref, sa_ref, sb_ref, lng_ref, lnb_ref, gng_ref, woa_ref,
                wob_ref, wout_ref, png_ref, pnb_ref, y_ref):
    h = _layer_norm(x_ref[...], lng_ref[...], lnb_ref[...])
    heads = []
    for hd in range(GLA_HEADS):
        cols = slice(hd * GLA_DV, (hd + 1) * GLA_DV)
        heads.append(_rms_norm(of_ref[:, cols].astype(F32) + ob_ref[:, cols].astype(F32), gng_ref[...]))
    a_in = (jnp.concatenate(heads, axis=1) * gr_ref[...].astype(F32)).astype(BF16)
    branch_a = _dot(a_in, woa_ref[...])
    branch_b = _dot(om_ref[...], wob_ref[...])
    mixed = sa_ref[...].astype(F32) * branch_a + sb_ref[...].astype(F32) * branch_b
    out = _dot(mixed.astype(BF16), wout_ref[...])
    y_ref[...] = _layer_norm(DEEPNORM_ALPHA * h + out, png_ref[...], pnb_ref[...])


def _out_stage(x2d, p, o_gla, o_mla, w, tm):
    n = x2d.shape[0]
    row = pl.BlockSpec((tm, D_MODEL), lambda i: (i, 0))
    consts = [w['emb_g'], w['emb_b'], w['gla_norm_g'], w['w_o_gla'], w['w_o_mla'], w['w_out'], w['post_g'],
              w['post_b']]
    o_fwd, o_bwd = o_gla
    return pl.pallas_call(
        _out_kernel,
        grid=(n // tm,),
        in_specs=[row] * 7 + [_const_spec(c.shape) for c in consts],
        out_specs=row,
        out_shape=jax.ShapeDtypeStruct((n, D_MODEL), F32),
        compiler_params=pltpu.CompilerParams(dimension_semantics=("arbitrary",), vmem_limit_bytes=VMEM_LIMIT),
        name="merge_out",
    )(x2d, o_fwd, o_bwd, p['gr'], o_mla, p['sa'], p['sb'], *consts)


def _rope_tables(start, length):
    inv_freq = 1.0 / (ROPE_THETA ** (jnp.arange(0, MLA_D_ROPE, 2, dtype=F32) / MLA_D_ROPE))
    ang = jnp.arange(start, start + length, dtype=F32)[:, None] * inv_freq[None, :]
    cos, sin = jnp.cos(ang), jnp.sin(ang)
    zero = jnp.zeros_like(cos)
    return (jnp.concatenate([cos, cos, zero, zero], axis=1),
            jnp.concatenate([zero, sin, zero, zero], axis=1),
            jnp.concatenate([-sin, zero, zero, zero], axis=1))


def _prepare_weights(emb_ln_g, emb_ln_b, w_in, b_merge, w_gla_gate_f, b_gla_gate_f, w_gla_gate_b, b_gla_gate_b,
                     gla_norm_g, w_o_gla, q_a_norm_g, w_q_b, kv_a_norm_g, w_kv_b, w_o_mla, w_out,
                     post_ln_g, post_ln_b):
    offs = [0]
    for s in IN_SPLITS:
        offs.append(offs[-1] + s)
    col = lambda i: w_in[0][:, offs[i]:offs[i + 1]]
    zcols = lambda n: jnp.zeros((D_MODEL, n), F32)
    w_sm = jnp.concatenate([col(4), col(5), zcols(LANES - 2 * GLA_GATE_RANK), col(6), col(7), col(8),
                            zcols(LANES - MLA_D_ROPE)], axis=1)
    w_gate = jnp.zeros((LANES, 2 * GLA_KEY_WIDTH), F32)
    w_gate = w_gate.at[:GLA_GATE_RANK, :GLA_KEY_WIDTH].set(w_gla_gate_f[0])
    w_gate = w_gate.at[GLA_GATE_RANK:2 * GLA_GATE_RANK, GLA_KEY_WIDTH:].set(w_gla_gate_b[0])
    wq = w_q_b[0].reshape(MLA_Q_RANK, MLA_HEADS, MLA_D_NOPE + MLA_D_ROPE)
    wq = jnp.pad(wq, ((0, 0), (0, 0), (0, MLA_QK_PAD - MLA_D_NOPE - MLA_D_ROPE)))
    wkv = w_kv_b[0].reshape(MLA_KV_RANK, MLA_HEADS, MLA_D_NOPE + MLA_D_V)
    wkv = jnp.concatenate([wkv[:, :, :MLA_D_NOPE].reshape(MLA_KV_RANK, -1),
                           wkv[:, :, MLA_D_NOPE:].reshape(MLA_KV_RANK, -1)], axis=1)
    r2 = lambda a: a.reshape(1, -1).astype(F32)
    return {
        'emb_g': r2(emb_ln_g), 'emb_b': r2(emb_ln_b),
        'w_qk': jnp.concatenate([col(0), col(1)], axis=1).astype(BF16),
        'w_v': col(2).astype(BF16), 'w_r': col(3).astype(BF16), 'w_z': col(9).astype(BF16),
        'w_a': col(10).astype(BF16), 'w_b': col(11).astype(BF16), 'w_sm': w_sm.astype(BF16),
        'b_ma': r2(b_merge[0][:D_MODEL]), 'b_mb': r2(b_merge[0][D_MODEL:]),
        'w_gate': w_gate.astype(BF16),
        'b_gate': r2(jnp.concatenate([b_gla_gate_f[0], b_gla_gate_b[0]])),
        'qn_g': r2(q_a_norm_g[0]), 'w_q': wq.reshape(MLA_Q_RANK, MLA_HEADS * MLA_QK_PAD).astype(BF16),
        'kvn_g': r2(kv_a_norm_g[0]), 'w_kv': wkv.astype(BF16),
        'gla_norm_g': r2(gla_norm_g[0]),
        'w_o_gla': w_o_gla[0].astype(BF16), 'w_o_mla': w_o_mla[0].astype(BF16), 'w_out': w_out[0].astype(BF16),
        'post_g': r2(post_ln_g[0]), 'post_b': r2(post_ln_b[0]),
    }


def _pick_tile(n, target):
    t = min(n, target)
    while n % t:
        t //= 2
    return t


def _encode(x, w, meta, s0, tm_in, tm_out, tq, tk):
    batch, seq, _ = x.shape
    x2d = x.reshape(batch * seq, D_MODEL)
    tm = _pick_tile(seq, tm_in)
    p = _inproj(x2d, tm, seq // tm, w, _rope_tables(N_META, seq), True)
    o_gla = _gla(p, s0, batch, seq)
    o_mla = _attention(p, meta, batch, seq, _pick_tile(seq, tq), _pick_tile(seq, tk))
    y = _out_stage(x2d, p, o_gla, o_mla, w, _pick_tile(seq, tm_out))
    return y.reshape(batch, seq, D_MODEL)


def kernel(x_prompt, x_sample, meta_tokens, emb_ln_g, emb_ln_b, w_in, b_merge, w_gla_gate_f, b_gla_gate_f, w_gla_gate_b, b_gla_gate_b, gla_norm_g, w_o_gla, q_a_norm_g, w_q_b, kv_a_norm_g, w_kv_b, w_o_mla, w_out, post_ln_g, post_ln_b):
    w = _prepare_weights(emb_ln_g, emb_ln_b, w_in, b_merge, w_gla_gate_f, b_gla_gate_f, w_gla_gate_b,
                         b_gla_gate_b, gla_norm_g, w_o_gla, q_a_norm_g, w_q_b, kv_a_norm_g, w_kv_b, w_o_mla,
                         w_out, post_ln_g, post_ln_b)
    meta = _inproj(meta_tokens.astype(F32), N_META, 1, w, _rope_tables(0, N_META), False)
    vt = meta['v'].reshape(N_META, MLA_HEADS, MLA_D_V).transpose(1, 2, 0)
    ones_row = jnp.zeros((MLA_HEADS, MLA_V_EXT - MLA_D_V, N_META), BF16).at[:, 0, :].set(1.0)
    meta['vt'] = jnp.concatenate([vt, ones_row], axis=1)
    lead = ((GLA_CHUNK - N_META, 0), (0, 0))
    s0 = _gla_meta_state(jnp.pad(meta['gk'], lead), jnp.pad(meta['gv'], lead), jnp.pad(meta['gf'], lead))
    y_prompt = _encode(x_prompt, w, meta, s0, 256, 512, 512, 512)
    y_sample = _encode(x_sample, w, meta, s0, 256, 512, 512, 512)
    return (y_prompt, y_sample)
```

```python
import functools

import jax
import jax.numpy as jnp
from jax import lax
from jax.experimental import pallas as pl
from jax.experimental.pallas import tpu as pltpu

D_MODEL = 1024
N_META = 16
GLA_HEADS = 4
GLA_DK = 128
GLA_DV = 256
GLA_KEY_WIDTH = GLA_HEADS * GLA_DK
GLA_WIDTH = GLA_HEADS * GLA_DV
GLA_GATE_RANK = 16
GLA_TAU = 16.0
GLA_CHUNK = 64
MLA_HEADS = 8
MLA_D_NOPE = 128
MLA_D_ROPE = 64
MLA_D_V = 128
MLA_Q_RANK = 384
MLA_KV_RANK = 256
MLA_WIDTH = MLA_HEADS * MLA_D_V
ROPE_THETA = 10000.0
LN_EPS = 1e-5
RMS_EPS = 1e-6
DEPTH = 1
DEEPNORM_ALPHA = (2 * DEPTH) ** 0.25
IN_SPLITS = (GLA_KEY_WIDTH, GLA_KEY_WIDTH, GLA_WIDTH, GLA_WIDTH, GLA_GATE_RANK, GLA_GATE_RANK,
             MLA_Q_RANK, MLA_KV_RANK, MLA_D_ROPE, MLA_WIDTH, D_MODEL, D_MODEL)

LANES = 128
MXU_COLS = 256
MLA_QK_PAD = 2 * LANES
BF16_SUBLANES = 16
MLA_V_EXT = MLA_D_V + BF16_SUBLANES
SMALL_COLS = LANES + MLA_Q_RANK + MLA_KV_RANK + LANES
VMEM_LIMIT = 56 * 1024 * 1024

LOG2_E = 1.4426950408889634
GLA_GROUP = 8
ATTN_SLOTS = 2
ATTN_UNROLL = 8

F32 = jnp.float32
BF16 = jnp.bfloat16
NT_DIMS = (((1,), (1,)), ((), ()))
TN_DIMS = (((0,), (0,)), ((), ()))


def _dot(a, b):
    return jnp.dot(a, b, preferred_element_type=F32)


def _sigmoid(x):
    return 1.0 / (1.0 + jnp.exp(-x))


def _layer_norm(x, g, b):
    mu = jnp.mean(x, axis=-1, keepdims=True)
    xc = x - mu
    var = jnp.mean(xc * xc, axis=-1, keepdims=True)
    return xc * lax.rsqrt(var + LN_EPS) * g + b


def _rms_norm(x, g):
    ms = jnp.mean(x * x, axis=-1, keepdims=True)
    return x * lax.rsqrt(ms + RMS_EPS) * g


def _rope(blk, c, s1, s2):
    return blk * c + pltpu.roll(blk, 32, 1) * s1 + pltpu.roll(blk, 96, 1) * s2


def _inproj_kernel(x_ref, lng_ref, lnb_ref, wqk_ref, wv_ref, wr_ref, wz_ref, wa_ref, wb_ref, wsm_ref,
                   bma_ref, bmb_ref, wgate_ref, bgate_ref, qng_ref, wq_ref, kvng_ref, wkv_ref,
                   rc_ref, rs1_ref, rs2_ref,
                   gq_ref, gk_ref, gv_ref, gr_ref, gf_ref, gb_ref, q_ref, k_ref, v_ref, mz_ref,
                   sa_ref, sb_ref, hb_ref, *, transposed):
    @pl.when(pl.program_id(0) == 0)
    def _():
        hb_ref[...] = jnp.zeros_like(hb_ref)

    def tiles(lhs, w_ref, emit):
        for j in range(w_ref.shape[0]):
            cols = slice(j * MXU_COLS, (j + 1) * MXU_COLS)
            emit(j, cols, _dot(lhs, w_ref[j]))

    def project(w_ref, emit):
        tiles(hb_ref[...], w_ref, emit)

    def split_store(lo_ref, hi_ref, f_lo, f_hi):
        half = lo_ref.shape[1] // MXU_COLS

        def emit(j, cols, t):
            if j < half:
                lo_ref[:, cols] = f_lo(t, cols)
            else:
                hi_ref[:, (j - half) * MXU_COLS:(j - half + 1) * MXU_COLS] = f_hi(t, cols)
        return emit

    def store(ref, f):
        def emit(j, cols, t):
            ref[:, cols] = f(t, cols)
        return emit

    p = _dot(hb_ref[...], wsm_ref[...])
    gl = p[:, :LANES].astype(BF16)
    cq = _rms_norm(p[:, LANES:LANES + MLA_Q_RANK], qng_ref[...]).astype(BF16)
    kv_lo = LANES + MLA_Q_RANK
    ckv = _rms_norm(p[:, kv_lo:kv_lo + MLA_KV_RANK], kvng_ref[...]).astype(BF16)
    kr_raw = p[:, kv_lo + MLA_KV_RANK:]

    as_bf16 = lambda t, cols: t.astype(BF16)
    project(wqk_ref, split_store(gq_ref, gk_ref, lambda t, cols: (t * (GLA_DK ** -0.5)).astype(BF16), as_bf16))

    def log_decay(t, cols):
        gx = t + bgate_ref[:, cols]
        softplus2 = jnp.log2(1.0 + jnp.exp2(jnp.abs(gx) * -LOG2_E))
        return (jnp.minimum(gx, 0.0) * LOG2_E - softplus2) * (1.0 / GLA_TAU)
    tiles(gl, wgate_ref, split_store(gf_ref, gb_ref, log_decay, log_decay))

    silu = lambda t, cols: (t * _sigmoid(t)).astype(BF16)
    project(wv_ref, store(gv_ref, as_bf16))
    project(wr_ref, store(gr_ref, silu))

    rc, rs1, rs2 = rc_ref[...], rs1_ref[...], rs2_ref[...]
    scale = (MLA_D_NOPE + MLA_D_ROPE) ** -0.5 * LOG2_E

    def emit_q(hd, cols, t):
        q_nope = t[:, :LANES] * scale
        q_rope = _rope(t[:, LANES:], rc, rs1, rs2) * scale
        if transposed:
            q_ref[hd, 0, :LANES, :] = q_nope.T.astype(BF16)
            q_ref[hd, 0, LANES:, :] = q_rope.T.astype(BF16)
        else:
            q_ref[hd, :, :LANES] = q_nope.astype(BF16)
            q_ref[hd, :, LANES:] = q_rope.astype(BF16)
    tiles(cq, wq_ref, emit_q)

    kr = _rope(kr_raw, rc, rs1, rs2).astype(BF16)
    tm = x_ref.shape[0]
    ones_row = (lax.broadcasted_iota(jnp.int32, (MLA_V_EXT - MLA_D_V, tm), 0) == 0).astype(BF16)
    heads_per_tile = MXU_COLS // LANES

    def emit_kv(j, cols, t):
        for u in range(heads_per_tile):
            part = t[:, u * LANES:(u + 1) * LANES]
            hd = (j * heads_per_tile + u) % MLA_HEADS
            if j * heads_per_tile + u < MLA_HEADS:
                k_ref[hd, :, :LANES] = part.astype(BF16)
                k_ref[hd, :, LANES:] = kr
            elif transposed:
                v_ref[hd, 0, :MLA_D_V, :] = part.T.astype(BF16)
                v_ref[hd, 0, MLA_D_V:, :] = ones_row
            else:
                v_ref[:, hd * MLA_D_V:(hd + 1) * MLA_D_V] = part.astype(BF16)
    tiles(ckv, wkv_ref, emit_kv)

    project(wz_ref, store(mz_ref, silu))
    project(wa_ref, store(sa_ref, lambda t, cols: _sigmoid(t + bma_ref[:, cols]).astype(BF16)))
    project(wb_ref, store(sb_ref, lambda t, cols: _sigmoid(t + bmb_ref[:, cols]).astype(BF16)))

    hb_ref[...] = _layer_norm(x_ref[...], lng_ref[...], lnb_ref[...]).astype(BF16)


def _const_spec(shape):
    nd = len(shape)
    return pl.BlockSpec(shape, lambda i: (0,) * nd, pipeline_mode=pl.Buffered(1))


def _inproj(x2d, tm, seq_tiles, w, rope_tabs, transposed):
    n = x2d.shape[0]
    n_tiles = n // tm
    prev = lambda i: jnp.maximum(i - 1, 0)
    row = lambda cols: pl.BlockSpec((tm, cols), lambda i: (prev(i), 0))
    tab = pl.BlockSpec((tm, LANES), lambda i: (prev(i) % seq_tiles, 0))
    consts = [w['emb_g'], w['emb_b'], w['w_qk'], w['w_v'], w['w_r'], w['w_z'], w['w_a'], w['w_b'], w['w_sm'],
              w['b_ma'], w['b_mb'], w['w_gate'], w['b_gate'], w['qn_g'], w['w_q'], w['kvn_g'], w['w_kv']]
    x_spec = pl.BlockSpec((tm, D_MODEL), lambda i: (jnp.minimum(i, n_tiles - 1), 0))
    in_specs = [x_spec] + [_const_spec(c.shape) for c in consts] + [tab, tab, tab]
    head_spec = pl.BlockSpec((MLA_HEADS, tm, MLA_QK_PAD), lambda i: (0, prev(i), 0))
    head_shape = jax.ShapeDtypeStruct((MLA_HEADS, n, MLA_QK_PAD), BF16)
    if transposed:
        q_spec = pl.BlockSpec((MLA_HEADS, 1, MLA_QK_PAD, tm), lambda i: (0, prev(i), 0, 0))
        q_shape = jax.ShapeDtypeStruct((MLA_HEADS, n_tiles, MLA_QK_PAD, tm), BF16)
        v_spec = pl.BlockSpec((MLA_HEADS, 1, MLA_V_EXT, tm), lambda i: (0, prev(i), 0, 0))
        v_shape = jax.ShapeDtypeStruct((MLA_HEADS, n_tiles, MLA_V_EXT, tm), BF16)
    else:
        q_spec, q_shape = head_spec, head_shape
        v_spec, v_shape = row(MLA_WIDTH), jax.ShapeDtypeStruct((n, MLA_WIDTH), BF16)
    out_shape = [
        jax.ShapeDtypeStruct((n, GLA_KEY_WIDTH), BF16), jax.ShapeDtypeStruct((n, GLA_KEY_WIDTH), BF16),
        jax.ShapeDtypeStruct((n, GLA_WIDTH), BF16), jax.ShapeDtypeStruct((n, GLA_WIDTH), BF16),
        jax.ShapeDtypeStruct((n, GLA_KEY_WIDTH), F32), jax.ShapeDtypeStruct((n, GLA_KEY_WIDTH), F32),
        q_shape, head_shape, v_shape, jax.ShapeDtypeStruct((n, MLA_WIDTH), BF16),
        jax.ShapeDtypeStruct((n, D_MODEL), BF16), jax.ShapeDtypeStruct((n, D_MODEL), BF16),
    ]
    out_specs = [row(GLA_KEY_WIDTH), row(GLA_KEY_WIDTH), row(GLA_WIDTH), row(GLA_WIDTH),
                 row(GLA_KEY_WIDTH), row(GLA_KEY_WIDTH), q_spec, head_spec,
                 v_spec, row(MLA_WIDTH), row(D_MODEL), row(D_MODEL)]
    outs = pl.pallas_call(
        functools.partial(_inproj_kernel, transposed=transposed),
        grid=(n_tiles + 1,),
        in_specs=in_specs,
        out_specs=out_specs,
        out_shape=out_shape,
        scratch_shapes=[pltpu.VMEM((tm, D_MODEL), BF16)],
        compiler_params=pltpu.CompilerParams(dimension_semantics=("arbitrary",), vmem_limit_bytes=VMEM_LIMIT),
        name="inproj",
    )(x2d, *consts, *rope_tabs)
    names = ('gq', 'gk', 'gv', 'gr', 'gf', 'gb', 'q', 'k', 'v', 'mz', 'sa', 'sb')
    return dict(zip(names, outs))


def _chunk_cumsum(tri, g):
    g_hi = g.astype(BF16)
    g_lo = (g - g_hi.astype(F32)).astype(BF16)
    r = _dot(tri, jnp.concatenate([g_hi, g_lo], axis=1))
    return r[:, :GLA_DK] + r[:, GLA_DK:]


def _chunk_consts():
    c = GLA_CHUNK
    r = lax.broadcasted_iota(jnp.int32, (c, c), 0)
    s = lax.broadcasted_iota(jnp.int32, (c, c), 1)
    rm = lax.broadcasted_iota(jnp.int32, (c, GLA_DK), 0)
    sm = lax.broadcasted_iota(jnp.int32, (c, GLA_DK), 1)
    return ((jnp.where(r >= s, 1.0, 0.0).astype(BF16), rm >= sm),
            (jnp.where(r <= s, 1.0, 0.0).astype(BF16), (rm < sm) & (sm < c)))


def _gla_prepare(q_ref, k_ref, items):
    c = GLA_CHUNK
    bcs = [_chunk_cumsum(it['tri'], it['g_ref'][pl.ds(it['rows'], c), :]) for it in items]
    pre = []
    for it, bc in zip(items, bcs):
        q = q_ref[pl.ds(it['rows'], c), :].astype(F32)
        k = k_ref[pl.ds(it['rows'], c), :].astype(F32)
        b_mid = bc[it['mid']:it['mid'] + 1]
        b_last = bc[it['last']:it['last'] + 1]
        qa = q * jnp.exp2(bc - b_mid)
        ka = k * jnp.exp2(b_mid - bc)
        pre.append(dict(
            qa=qa.astype(BF16), ka=ka.astype(BF16),
            q_in=(qa * jnp.exp2(b_mid)).astype(BF16),
            k_dec=(ka * jnp.exp2(b_last - b_mid)).astype(BF16),
            dcol=jnp.transpose(jnp.broadcast_to(jnp.exp2(b_last), (GLA_DK, GLA_DK)))))
    return pre


def _gla_apply(v_ref, items, pre, states):
    c = GLA_CHUNK
    zero_k = jnp.zeros((GLA_DK - c, GLA_DK), BF16)
    zero_v = jnp.zeros((GLA_DK - c, GLA_DV), BF16)
    vs = [v_ref[pl.ds(it['rows'], c), :] for it in items]
    a_list = [lax.dot_general(p['qa'], jnp.concatenate([p['ka'], zero_k], axis=0), NT_DIMS,
                              preferred_element_type=F32) for p in pre]
    kv_list = [lax.dot_general(p['k_dec'], v, TN_DIMS, preferred_element_type=F32) for p, v in zip(pre, vs)]
    states = list(states)
    for it, p, v, a, kv in zip(items, pre, vs, a_list, kv_list):
        s = states[it['dirn']]
        lhs = jnp.concatenate([p['q_in'], jnp.where(it['mask'], a, 0.0).astype(BF16)], axis=1)
        rhs = jnp.concatenate([s.astype(BF16), v, zero_v], axis=0)
        it['out_ref'][pl.ds(it['rows'], c), :] = _dot(lhs, rhs).astype(BF16)
        states[it['dirn']] = s * jnp.concatenate([p['dcol'], p['dcol']], axis=1) + kv
    return states


def _gla_kernel(q_ref, k_ref, v_ref, gf_ref, gb_ref, s0_ref, of_ref, ob_ref, sf_ref, sb_ref, *, n_chunks, group):
    c = GLA_CHUNK
    (tri_f, mask_f), (tri_b, mask_b) = _chunk_consts()
    sf_ref[...] = s0_ref[0]
    sb_ref[...] = jnp.zeros_like(sb_ref)

    def body(i, carry):
        items = []
        for u in range(group):
            items.append(dict(rows=pl.multiple_of((i * group + u) * c, c), g_ref=gf_ref, tri=tri_f, mask=mask_f,
                              mid=c // 2, last=c - 1, dirn=0, out_ref=of_ref))
        for u in range(group):
            items.append(dict(rows=pl.multiple_of((n_chunks - 1 - i * group - u) * c, c), g_ref=gb_ref, tri=tri_b,
                              mask=mask_b, mid=c // 2 - 1, last=0, dirn=1, out_ref=ob_ref))
        pre = _gla_prepare(q_ref, k_ref, items)
        s_f, s_b = _gla_apply(v_ref, items, pre, [sf_ref[...], sb_ref[...]])
        sf_ref[...] = s_f
        sb_ref[...] = s_b
        return carry

    lax.fori_loop(0, n_chunks // group, body, 0)


def _gla(p, s0, batch, seq):
    n_chunks = seq // GLA_CHUNK
    group = _pick_tile(n_chunks, GLA_GROUP)
    kspec = pl.BlockSpec((seq, GLA_DK), lambda b, h: (b, h))
    vspec = pl.BlockSpec((seq, GLA_DV), lambda b, h: (b, h))
    out = jax.ShapeDtypeStruct((batch * seq, GLA_WIDTH), BF16)
    return pl.pallas_call(
        functools.partial(_gla_kernel, n_chunks=n_chunks, group=group),
        grid=(batch, GLA_HEADS),
        in_specs=[kspec, kspec, vspec, kspec, kspec,
                  pl.BlockSpec((1, GLA_DK, GLA_DV), lambda b, h: (h, 0, 0))],
        out_specs=[vspec, vspec],
        out_shape=[out, out],
        scratch_shapes=[pltpu.VMEM((GLA_DK, GLA_DV), F32), pltpu.VMEM((GLA_DK, GLA_DV), F32)],
        compiler_params=pltpu.CompilerParams(dimension_semantics=("arbitrary", "arbitrary"),
                                             vmem_limit_bytes=VMEM_LIMIT),
        name="gla",
    )(p['gq'], p['gk'], p['gv'], p['gf'], p['gb'], s0)


def _gla_meta_state_kernel(k_ref, v_ref, g_ref, s_ref):
    (tri_f, _), _ = _chunk_consts()
    bc = _chunk_cumsum(tri_f, g_ref[...])
    k_dec = (k_ref[...].astype(F32) * jnp.exp2(bc[GLA_CHUNK - 1:GLA_CHUNK] - bc)).astype(BF16)
    s_ref[0] = lax.dot_general(k_dec, v_ref[...], TN_DIMS, preferred_element_type=F32)


def _gla_meta_state(k_pad, v_pad, g_pad):
    return pl.pallas_call(
        _gla_meta_state_kernel,
        grid=(GLA_HEADS,),
        in_specs=[pl.BlockSpec((GLA_CHUNK, GLA_DK), lambda h: (0, h)),
                  pl.BlockSpec((GLA_CHUNK, GLA_DV), lambda h: (0, h)),
                  pl.BlockSpec((GLA_CHUNK, GLA_DK), lambda h: (0, h))],
        out_specs=pl.BlockSpec((1, GLA_DK, GLA_DV), lambda h: (h, 0, 0)),
        out_shape=jax.ShapeDtypeStruct((GLA_HEADS, GLA_DK, GLA_DV), F32),
        name="gla_meta_state",
    )(k_pad, v_pad, g_pad)


def _attn_kernel(qt_ref, k_ref, vt_ref, km_ref, vmt_ref, z_ref, o_ref, acc_ref, m0_ref, s_ref, p_ref, *,
                 tk, n_q, n_kv, unroll):
    n_split = acc_ref.shape[1]
    hw = acc_ref.shape[3]
    kc = tk // hw
    n_tiles = n_q * n_kv

    def stage_a(t, slot):
        qi, j = t // n_kv, t % n_kv
        tile_max = []
        for a in range(n_split):
            k_j = k_ref[0, pl.ds(pl.multiple_of(j * tk, tk), tk), :]
            s = _dot(k_j, qt_ref[0, qi * n_split + a])
            s_ref[slot, a] = s
            tile_max.append(jnp.max(s, axis=0, keepdims=True))
        return tuple(tile_max)

    def stage_b(t, slot, tile_max, ms):
        qi, j = t // n_kv, t % n_kv
        new_ms, alphas = [], []
        for a in range(n_split):
            m_old = jnp.where(j == 0, m0_ref[qi, a], ms[a])
            m_new = jnp.maximum(m_old, tile_max[a])
            alphas.append(jnp.exp2(m_old - m_new))
            p_ref[slot, a] = jnp.exp2(s_ref[slot, a] - m_new).astype(BF16)
            new_ms.append(m_new)
        return tuple(new_ms), tuple(alphas)

    def stage_c(t, slot, alphas):
        qi, j = t // n_kv, t % n_kv
        for a in range(n_split):
            pv = _dot(vt_ref[0, j * kc], p_ref[slot, a, :hw, :])
            for c in range(1, kc):
                pv += _dot(vt_ref[0, j * kc + c], p_ref[slot, a, c * hw:(c + 1) * hw, :])
            acc_ref[qi, a] = alphas[a] * acc_ref[qi, a] + pv

    def step(t, slot, carry):
        ms, alphas, max_next = carry
        ms, new_alphas = stage_b(t + 1, (slot + 1) % ATTN_SLOTS, max_next, ms)
        max_after = stage_a(t + 2, (slot + 2) % ATTN_SLOTS)
        stage_c(t, slot, alphas)
        return ms, new_alphas, max_after

    groups = range(n_q * n_split)
    s_meta = [_dot(km_ref[0], qt_ref[0, g]) for g in groups]
    m_meta = [jnp.max(s, axis=0, keepdims=True) for s in s_meta]
    p_meta = [jnp.exp2(s - m).astype(BF16) for s, m in zip(s_meta, m_meta)]
    for g in groups:
        m0_ref[g // n_split, g % n_split] = m_meta[g]
        acc_ref[g // n_split, g % n_split] = _dot(vmt_ref[0], p_meta[g])

    max_next = stage_a(0, 0)
    ms, alphas = stage_b(0, 0, max_next, tuple(m_meta[:n_split]))
    carry = (ms, alphas, stage_a(1, 1))

    n_steps = n_tiles - 2
    n_loops = n_steps // unroll

    def body(i, carry):
        for u in range(unroll):
            carry = step(i * unroll + u, u % ATTN_SLOTS, carry)
        return carry

    carry = lax.fori_loop(0, n_loops, body, carry)
    for t in range(n_loops * unroll, n_steps):
        carry = step(t, t % ATTN_SLOTS, carry)
    ms, alphas, max_next = carry
    stage_c(n_tiles - 2, (n_tiles - 2) % ATTN_SLOTS, alphas)
    ms, alphas = stage_b(n_tiles - 1, (n_tiles - 1) % ATTN_SLOTS, max_next, ms)
    stage_c(n_tiles - 1, (n_tiles - 1) % ATTN_SLOTS, alphas)

    for g in groups:
        acc = acc_ref[g // n_split, g % n_split]
        o = jnp.transpose(acc[:MLA_D_V] * (1.0 / acc[MLA_D_V:MLA_D_V + 1]))
        rows = slice(g * hw, (g + 1) * hw)
        o_ref[rows, :] = (o * z_ref[rows, :].astype(F32)).astype(BF16)


def _attention(p, meta, batch, seq, tq, tk):
    hw = p['q'].shape[3]
    n_split, n_q, n_kv = tq // hw, seq // tq, seq // tk
    assert tq % hw == 0 and tk % hw == 0 and n_q * n_kv >= 3
    return pl.pallas_call(
        functools.partial(_attn_kernel, tk=tk, n_q=n_q, n_kv=n_kv, unroll=ATTN_UNROLL),
        grid=(batch, MLA_HEADS),
        in_specs=[pl.BlockSpec((1, seq // hw, MLA_QK_PAD, hw), lambda b, h: (h, b, 0, 0)),
                  pl.BlockSpec((1, seq, MLA_QK_PAD), lambda b, h: (h, b, 0)),
                  pl.BlockSpec((1, seq // hw, MLA_V_EXT, hw), lambda b, h: (h, b, 0, 0)),
                  pl.BlockSpec((1, N_META, MLA_QK_PAD), lambda b, h: (h, 0, 0)),
                  pl.BlockSpec((1, MLA_V_EXT, N_META), lambda b, h: (h, 0, 0)),
                  pl.BlockSpec((seq, MLA_D_V), lambda b, h: (b, h))],
        out_specs=pl.BlockSpec((seq, MLA_D_V), lambda b, h: (b, h)),
        out_shape=jax.ShapeDtypeStruct((batch * seq, MLA_WIDTH), BF16),
        scratch_shapes=[pltpu.VMEM((n_q, n_split, MLA_V_EXT, hw), F32),
                        pltpu.VMEM((n_q, n_split, 1, hw), F32),
                        pltpu.VMEM((ATTN_SLOTS, n_split, tk, hw), F32),
                        pltpu.VMEM((ATTN_SLOTS, n_split, tk, hw), BF16)],
        compiler_params=pltpu.CompilerParams(dimension_semantics=("arbitrary", "arbitrary"),
                                             vmem_limit_bytes=VMEM_LIMIT),
        name="mla_attention",
    )(p['q'], p['k'], p['v'], meta['k'], meta['vt'], p['mz'])


def _out_kernel(x_ref, of_ref, ob_ref, gr_ref, om_ref, sa_ref, sb_ref, lng_ref, lnb_ref, gng_ref, woa_ref,
                wob_ref, wout_ref, png_ref, pnb_ref, y_ref):
    h = _layer_norm(x_ref[...], lng_ref[...], lnb_ref[...])
    heads = []
    for hd in range(GLA_HEADS):
        cols = slice(hd * GLA_DV, (hd + 1) * GLA_DV)
        heads.append(_rms_norm(of_ref[:, cols].astype(F32) + ob_ref[:, cols].astype(F32), gng_ref[...]))
    a_in = (jnp.concatenate(heads, axis=1) * gr_ref[...].astype(F32)).astype(BF16)
    branch_a = _dot(a_in, woa_ref[...])
    branch_b = _dot(om_ref[...], wob_ref[...])
    mixed = sa_ref[...].astype(F32) * branch_a + sb_ref[...].astype(F32) * branch_b
    out = _dot(mixed.astype(BF16), wout_ref[...])
    y_ref[...] = _layer_norm(DEEPNORM_ALPHA * h + out, png_ref[...], pnb_ref[...])


def _out_stage(x2d, p, o_gla, o_mla, w, tm):
    n = x2d.shape[0]
    row = pl.BlockSpec((tm, D_MODEL), lambda i: (i, 0))
    consts = [w['emb_g'], w['emb_b'], w['gla_norm_g'], w['w_o_gla'], w['w_o_mla'], w['w_out'], w['post_g'],
              w['post_b']]
    o_fwd, o_bwd = o_gla
    return pl.pallas_call(
        _out_kernel,
        grid=(n // tm,),
        in_specs=[row] * 7 + [_const_spec(c.shape) for c in consts],
        out_specs=row,
        out_shape=jax.ShapeDtypeStruct((n, D_MODEL), F32),
        compiler_params=pltpu.CompilerParams(dimension_semantics=("arbitrary",), vmem_limit_bytes=VMEM_LIMIT),
        name="merge_out",
    )(x2d, o_fwd, o_bwd, p['gr'], o_mla, p['sa'], p['sb'], *consts)


def _rope_tables(start, length):
    inv_freq = 1.0 / (ROPE_THETA ** (jnp.arange(0, MLA_D_ROPE, 2, dtype=F32) / MLA_D_ROPE))
    ang = jnp.arange(start, start + length, dtype=F32)[:, None] * inv_freq[None, :]
    cos, sin = jnp.cos(ang), jnp.sin(ang)
    zero = jnp.zeros_like(cos)
    return (jnp.concatenate([cos, cos, zero, zero], axis=1),
            jnp.concatenate([zero, sin, zero, zero], axis=1),
            jnp.concatenate([-sin, zero, zero, zero], axis=1))


def _prepare_weights(emb_ln_g, emb_ln_b, w_in, b_merge, w_gla_gate_f, b_gla_gate_f, w_gla_gate_b, b_gla_gate_b,
                     gla_norm_g, w_o_gla, q_a_norm_g, w_q_b, kv_a_norm_g, w_kv_b, w_o_mla, w_out,
                     post_ln_g, post_ln_b):
    offs = [0]
    for s in IN_SPLITS:
        offs.append(offs[-1] + s)
    col = lambda i: w_in[0][:, offs[i]:offs[i + 1]]
    zcols = lambda n: jnp.zeros((D_MODEL, n), F32)
    w_sm = jnp.concatenate([col(4), col(5), zcols(LANES - 2 * GLA_GATE_RANK), col(6), col(7), col(8),
                            zcols(LANES - MLA_D_ROPE)], axis=1)
    w_gate = jnp.zeros((LANES, 2 * GLA_KEY_WIDTH), F32)
    w_gate = w_gate.at[:GLA_GATE_RANK, :GLA_KEY_WIDTH].set(w_gla_gate_f[0])
    w_gate = w_gate.at[GLA_GATE_RANK:2 * GLA_GATE_RANK, GLA_KEY_WIDTH:].set(w_gla_gate_b[0])
    wq = w_q_b[0].reshape(MLA_Q_RANK, MLA_HEADS, MLA_D_NOPE + MLA_D_ROPE)
    wq = jnp.pad(wq, ((0, 0), (0, 0), (0, MLA_QK_PAD - MLA_D_NOPE - MLA_D_ROPE)))
    wkv = w_kv_b[0].reshape(MLA_KV_RANK, MLA_HEADS, MLA_D_NOPE + MLA_D_V)
    wkv = jnp.concatenate([wkv[:, :, :MLA_D_NOPE].reshape(MLA_KV_RANK, -1),
                           wkv[:, :, MLA_D_NOPE:].reshape(MLA_KV_RANK, -1)], axis=1)
    r2 = lambda a: a.reshape(1, -1).astype(F32)

    def col_tiles(a):
        k, n = a.shape
        return a.astype(BF16).reshape(k, n // MXU_COLS, MXU_COLS).transpose(1, 0, 2)

    return {
        'emb_g': r2(emb_ln_g), 'emb_b': r2(emb_ln_b),
        'w_qk': col_tiles(jnp.concatenate([col(0), col(1)], axis=1)),
        'w_v': col_tiles(col(2)), 'w_r': col_tiles(col(3)), 'w_z': col_tiles(col(9)),
        'w_a': col_tiles(col(10)), 'w_b': col_tiles(col(11)), 'w_sm': w_sm.astype(BF16),
        'b_ma': r2(b_merge[0][:D_MODEL]), 'b_mb': r2(b_merge[0][D_MODEL:]),
        'w_gate': col_tiles(w_gate),
        'b_gate': r2(jnp.concatenate([b_gla_gate_f[0], b_gla_gate_b[0]])),
        'qn_g': r2(q_a_norm_g[0]), 'w_q': col_tiles(wq.reshape(MLA_Q_RANK, MLA_HEADS * MLA_QK_PAD)),
        'kvn_g': r2(kv_a_norm_g[0]), 'w_kv': col_tiles(wkv),
        'gla_norm_g': r2(gla_norm_g[0]),
        'w_o_gla': w_o_gla[0].astype(BF16), 'w_o_mla': w_o_mla[0].astype(BF16), 'w_out': w_out[0].astype(BF16),
        'post_g': r2(post_ln_g[0]), 'post_b': r2(post_ln_b[0]),
    }


def _pick_tile(n, target):
    t = min(n, target)
    while n % t:
        t //= 2
    return t


def _encode(x, w, meta, s0, tm_in, tm_out, tq, tk):
    batch, seq, _ = x.shape
    x2d = x.reshape(batch * seq, D_MODEL)
    tm = _pick_tile(seq, tm_in)
    p = _inproj(x2d, tm, seq // tm, w, _rope_tables(N_META, seq), True)
    o_gla = _gla(p, s0, batch, seq)
    o_mla = _attention(p, meta, batch, seq, _pick_tile(seq, tq), _pick_tile(seq, tk))
    y = _out_stage(x2d, p, o_gla, o_mla, w, _pick_tile(seq, tm_out))
    return y.reshape(batch, seq, D_MODEL)


def kernel(x_prompt, x_sample, meta_tokens, emb_ln_g, emb_ln_b, w_in, b_merge, w_gla_gate_f, b_gla_gate_f, w_gla_gate_b, b_gla_gate_b, gla_norm_g, w_o_gla, q_a_norm_g, w_q_b, kv_a_norm_g, w_kv_b, w_o_mla, w_out, post_ln_g, post_ln_b):
    w = _prepare_weights(emb_ln_g, emb_ln_b, w_in, b_merge, w_gla_gate_f, b_gla_gate_f, w_gla_gate_b,
                         b_gla_gate_b, gla_norm_g, w_o_gla, q_a_norm_g, w_q_b, kv_a_norm_g, w_kv_b, w_o_mla,
                         w_out, post_ln_g, post_ln_b)
    meta = _inproj(meta_tokens.astype(F32), N_META, 1, w, _rope_tables(0, N_META), False)
    vt = meta['v'].reshape(N_META, MLA_HEADS, MLA_D_V).transpose(1, 2, 0)
    ones_row = jnp.zeros((MLA_HEADS, MLA_V_EXT - MLA_D_V, N_META), BF16).at[:, 0, :].set(1.0)
    meta['vt'] = jnp.concatenate([vt, ones_row], axis=1)
    lead = ((GLA_CHUNK - N_META, 0), (0, 0))
    s0 = _gla_meta_state(jnp.pad(meta['gk'], lead), jnp.pad(meta['gv'], lead), jnp.pad(meta['gf'], lead))
    y_prompt = _encode(x_prompt, w, meta, s0, 256, 512, 512, 512)
    y_sample = _encode(x_sample, w, meta, s0, 256, 512, 512, 512)
    return (y_prompt, y_sample)
```

```python
import functools

import jax
import jax.numpy as jnp
from jax import lax
from jax.experimental import pallas as pl
from jax.experimental.pallas import tpu as pltpu

D_MODEL = 1024
N_META = 16
GLA_HEADS = 4
GLA_DK = 128
GLA_DV = 256
GLA_KEY_WIDTH = GLA_HEADS * GLA_DK
GLA_WIDTH = GLA_HEADS * GLA_DV
GLA_GATE_RANK = 16
GLA_TAU = 16.0
GLA_CHUNK = 64
MLA_HEADS = 8
MLA_D_NOPE = 128
MLA_D_ROPE = 64
MLA_D_V = 128
MLA_Q_RANK = 384
MLA_KV_RANK = 256
MLA_WIDTH = MLA_HEADS * MLA_D_V
ROPE_THETA = 10000.0
LN_EPS = 1e-5
RMS_EPS = 1e-6
DEPTH = 1
DEEPNORM_ALPHA = (2 * DEPTH) ** 0.25
IN_SPLITS = (GLA_KEY_WIDTH, GLA_KEY_WIDTH, GLA_WIDTH, GLA_WIDTH, GLA_GATE_RANK, GLA_GATE_RANK,
             MLA_Q_RANK, MLA_KV_RANK, MLA_D_ROPE, MLA_WIDTH, D_MODEL, D_MODEL)

LANES = 128
MXU_COLS = 256
MLA_QK_PAD = 2 * LANES
BF16_SUBLANES = 16
MLA_V_EXT = MLA_D_V + BF16_SUBLANES
SMALL_COLS = LANES + MLA_Q_RANK + MLA_KV_RANK
VMEM_LIMIT = 56 * 1024 * 1024

LOG2_E = 1.4426950408889634
INPROJ_ROWS = 256
MERGE_ROWS = 512
ATTN_Q_ROWS = 512
ATTN_K_ROWS = 1024
GLA_GROUP = 16
ATTN_SLOTS = 2
ATTN_UNROLL = 8

F32 = jnp.float32
BF16 = jnp.bfloat16
NT_DIMS = (((1,), (1,)), ((), ()))
TN_DIMS = (((0,), (0,)), ((), ()))


def _dot(a, b):
    return jnp.dot(a, b, preferred_element_type=F32)


def _sigmoid(x):
    return 1.0 / (1.0 + jnp.exp(-x))


def _layer_norm(x, g, b):
    mu = jnp.mean(x, axis=-1, keepdims=True)
    xc = x - mu
    var = jnp.mean(xc * xc, axis=-1, keepdims=True)
    return xc * lax.rsqrt(var + LN_EPS) * g + b


def _rms_norm(x, g):
    ms = jnp.mean(x * x, axis=-1, keepdims=True)
    return x * lax.rsqrt(ms + RMS_EPS) * g


def _rope(blk, c, s1, s2):
    return blk * c + pltpu.roll(blk, 32, 1) * s1 + pltpu.roll(blk, 96, 1) * s2


def _inproj_kernel(x_ref, lng_ref, lnb_ref, wqk_ref, wv_ref, wr_ref, wz_ref, wa_ref, wb_ref, wsm_ref,
                   bma_ref, bmb_ref, wgate_ref, bgate_ref, qng_ref, wq_ref, kvng_ref, wkv_ref,
                   rc_ref, rs1_ref, rs2_ref, rqc_ref, rqs1_ref, rqs2_ref,
                   gq_ref, gk_ref, gv_ref, gr_ref, gf_ref, gb_ref, q_ref, k_ref, v_ref, mz_ref,
                   sa_ref, sb_ref, hb_ref, *, transposed):
    @pl.when(pl.program_id(0) == 0)
    def _():
        hb_ref[...] = jnp.zeros_like(hb_ref)

    def tiles(lhs, w_ref, emit):
        for j in range(w_ref.shape[0]):
            cols = slice(j * MXU_COLS, (j + 1) * MXU_COLS)
            emit(j, cols, _dot(lhs, w_ref[j]))

    def project(w_ref, emit):
        tiles(hb_ref[...], w_ref, emit)

    def split_store(lo_ref, hi_ref, f_lo, f_hi):
        half = lo_ref.shape[1] // MXU_COLS

        def emit(j, cols, t):
            if j < half:
                lo_ref[:, cols] = f_lo(t, cols)
            else:
                hi_ref[:, (j - half) * MXU_COLS:(j - half + 1) * MXU_COLS] = f_hi(t, cols)
        return emit

    def store(ref, f):
        def emit(j, cols, t):
            ref[:, cols] = f(t, cols)
        return emit

    p = _dot(hb_ref[...], wsm_ref[...])
    blk0 = p[:, :LANES]
    gl = blk0.astype(BF16)
    cq = _rms_norm(p[:, LANES:LANES + MLA_Q_RANK], qng_ref[...]).astype(BF16)
    kv_lo = LANES + MLA_Q_RANK
    ckv = _rms_norm(p[:, kv_lo:kv_lo + MLA_KV_RANK], kvng_ref[...]).astype(BF16)

    as_bf16 = lambda t, cols: t.astype(BF16)
    project(wqk_ref, split_store(gq_ref, gk_ref, lambda t, cols: (t * (GLA_DK ** -0.5)).astype(BF16), as_bf16))

    def log_decay(t, cols):
        gx = t + bgate_ref[:, cols]
        softplus2 = jnp.log2(1.0 + jnp.exp2(jnp.abs(gx) * -LOG2_E))
        return (jnp.minimum(gx, 0.0) * LOG2_E - softplus2) * (1.0 / GLA_TAU)
    tiles(gl, wgate_ref, split_store(gf_ref, gb_ref, log_decay, log_decay))

    silu = lambda t, cols: (t * _sigmoid(t)).astype(BF16)
    project(wv_ref, store(gv_ref, as_bf16))
    project(wr_ref, store(gr_ref, silu))

    rc, rs1, rs2 = rc_ref[...], rs1_ref[...], rs2_ref[...]
    scale = (MLA_D_NOPE + MLA_D_ROPE) ** -0.5 * LOG2_E

    tm = x_ref.shape[0]
    nope_tiles = MLA_HEADS * MLA_D_NOPE // MXU_COLS
    rope_end = MLA_D_NOPE + MLA_D_ROPE

    def emit_q(j, cols, t):
        for u in range(MXU_COLS // LANES):
            part = t[:, u * LANES:(u + 1) * LANES]
            if j < nope_tiles:
                q_ref[2 * j + u, 0, :MLA_D_NOPE, :] = (part * scale).T.astype(BF16)
            else:
                pair = (_rope(part, rqc_ref[...], rqs1_ref[...], rqs2_ref[...]) * scale).T.astype(BF16)
                hd = 4 * (j - nope_tiles) + 2 * u
                q_ref[hd, 0, MLA_D_NOPE:rope_end, :] = pair[:MLA_D_ROPE]
                q_ref[hd + 1, 0, MLA_D_NOPE:rope_end, :] = pair[MLA_D_ROPE:]

    if transposed:
        tiles(cq, wq_ref, emit_q)
        q_ref[:, 0, rope_end:, :] = jnp.zeros((MLA_HEADS, MLA_QK_PAD - rope_end, tm), BF16)
    else:
        q_ref[...] = jnp.zeros_like(q_ref)

    kr = _rope(blk0, rc, rs1, rs2).astype(BF16)
    ones_row = (lax.broadcasted_iota(jnp.int32, (MLA_V_EXT - MLA_D_V, tm), 0) == 0).astype(BF16)
    heads_per_tile = MXU_COLS // LANES

    def emit_kv(j, cols, t):
        for u in range(heads_per_tile):
            part = t[:, u * LANES:(u + 1) * LANES]
            hd = (j * heads_per_tile + u) % MLA_HEADS
            if j * heads_per_tile + u < MLA_HEADS:
                k_ref[hd, :, :LANES] = part.astype(BF16)
                k_ref[hd, :, LANES:] = kr
            elif transposed:
                v_ref[hd, 0, :MLA_D_V, :] = part.T.astype(BF16)
                v_ref[hd, 0, MLA_D_V:, :] = ones_row
            else:
                v_ref[:, hd * MLA_D_V:(hd + 1) * MLA_D_V] = part.astype(BF16)
    tiles(ckv, wkv_ref, emit_kv)

    project(wz_ref, store(mz_ref, silu))
    project(wa_ref, store(sa_ref, lambda t, cols: _sigmoid(t + bma_ref[:, cols]).astype(BF16)))
    project(wb_ref, store(sb_ref, lambda t, cols: _sigmoid(t + bmb_ref[:, cols]).astype(BF16)))

    hb_ref[...] = _layer_norm(x_ref[...], lng_ref[...], lnb_ref[...]).astype(BF16)


def _const_spec(shape):
    nd = len(shape)
    return pl.BlockSpec(shape, lambda i: (0,) * nd, pipeline_mode=pl.Buffered(1))


def _inproj(x2d, tm, seq_tiles, w, rope_tabs, transposed):
    n = x2d.shape[0]
    n_tiles = n // tm
    prev = lambda i: jnp.maximum(i - 1, 0)
    row = lambda cols: pl.BlockSpec((tm, cols), lambda i: (prev(i), 0))
    tab = pl.BlockSpec((tm, LANES), lambda i: (prev(i) % seq_tiles, 0))
    consts = [w['emb_g'], w['emb_b'], w['w_qk'], w['w_v'], w['w_r'], w['w_z'], w['w_a'], w['w_b'], w['w_sm'],
              w['b_ma'], w['b_mb'], w['w_gate'], w['b_gate'], w['qn_g'], w['w_q'], w['kvn_g'], w['w_kv']]
    x_spec = pl.BlockSpec((tm, D_MODEL), lambda i: (jnp.minimum(i, n_tiles - 1), 0))
    in_specs = [x_spec] + [_const_spec(c.shape) for c in consts] + [tab] * len(rope_tabs)
    head_spec = pl.BlockSpec((MLA_HEADS, tm, MLA_QK_PAD), lambda i: (0, prev(i), 0))
    head_shape = jax.ShapeDtypeStruct((MLA_HEADS, n, MLA_QK_PAD), BF16)
    if transposed:
        q_spec = pl.BlockSpec((MLA_HEADS, 1, MLA_QK_PAD, tm), lambda i: (0, prev(i), 0, 0))
        q_shape = jax.ShapeDtypeStruct((MLA_HEADS, n_tiles, MLA_QK_PAD, tm), BF16)
        v_spec = pl.BlockSpec((MLA_HEADS, 1, MLA_V_EXT, tm), lambda i: (0, prev(i), 0, 0))
        v_shape = jax.ShapeDtypeStruct((MLA_HEADS, n_tiles, MLA_V_EXT, tm), BF16)
    else:
        q_spec, q_shape = head_spec, head_shape
        v_spec, v_shape = row(MLA_WIDTH), jax.ShapeDtypeStruct((n, MLA_WIDTH), BF16)
    out_shape = [
        jax.ShapeDtypeStruct((n, GLA_KEY_WIDTH), BF16), jax.ShapeDtypeStruct((n, GLA_KEY_WIDTH), BF16),
        jax.ShapeDtypeStruct((n, GLA_WIDTH), BF16), jax.ShapeDtypeStruct((n, GLA_WIDTH), BF16),
        jax.ShapeDtypeStruct((n, GLA_KEY_WIDTH), F32), jax.ShapeDtypeStruct((n, GLA_KEY_WIDTH), F32),
        q_shape, head_shape, v_shape, jax.ShapeDtypeStruct((n, MLA_WIDTH), BF16),
        jax.ShapeDtypeStruct((n, D_MODEL), BF16), jax.ShapeDtypeStruct((n, D_MODEL), BF16),
    ]
    out_specs = [row(GLA_KEY_WIDTH), row(GLA_KEY_WIDTH), row(GLA_WIDTH), row(GLA_WIDTH),
                 row(GLA_KEY_WIDTH), row(GLA_KEY_WIDTH), q_spec, head_spec,
                 v_spec, row(MLA_WIDTH), row(D_MODEL), row(D_MODEL)]
    outs = pl.pallas_call(
        functools.partial(_inproj_kernel, transposed=transposed),
        grid=(n_tiles + 1,),
        in_specs=in_specs,
        out_specs=out_specs,
        out_shape=out_shape,
        scratch_shapes=[pltpu.VMEM((tm, D_MODEL), BF16)],
        compiler_params=pltpu.CompilerParams(dimension_semantics=("arbitrary",), vmem_limit_bytes=VMEM_LIMIT),
        name="inproj",
    )(x2d, *consts, *rope_tabs)
    names = ('gq', 'gk', 'gv', 'gr', 'gf', 'gb', 'q', 'k', 'v', 'mz', 'sa', 'sb')
    return dict(zip(names, outs))


def _chunk_cumsum(tri, g):
    g_hi = g.astype(BF16)
    g_lo = (g - g_hi.astype(F32)).astype(BF16)
    r = _dot(tri, jnp.concatenate([g_hi, g_lo], axis=1))
    return r[:, :GLA_DK] + r[:, GLA_DK:]


def _chunk_consts():
    c = GLA_CHUNK
    r = lax.broadcasted_iota(jnp.int32, (c, c), 0)
    s = lax.broadcasted_iota(jnp.int32, (c, c), 1)
    rm = lax.broadcasted_iota(jnp.int32, (c, GLA_DK), 0)
    sm = lax.broadcasted_iota(jnp.int32, (c, GLA_DK), 1)
    return ((jnp.where(r >= s, 1.0, 0.0).astype(BF16), rm >= sm),
            (jnp.where(r <= s, 1.0, 0.0).astype(BF16), (rm < sm) & (sm < c)))


def _gla_prepare(q_ref, k_ref, items):
    c = GLA_CHUNK
    bcs = [_chunk_cumsum(it['tri'], it['g_ref'][pl.ds(it['rows'], c), :]) for it in items]
    pre = []
    for it, bc in zip(items, bcs):
        q = q_ref[pl.ds(it['rows'], c), :].astype(F32)
        k = k_ref[pl.ds(it['rows'], c), :].astype(F32)
        b_mid = bc[it['mid']:it['mid'] + 1]
        b_last = bc[it['last']:it['last'] + 1]
        qa = q * jnp.exp2(bc - b_mid)
        ka = k * jnp.exp2(b_mid - bc)
        pre.append(dict(
            qa=qa.astype(BF16), ka=ka.astype(BF16),
            q_in=(qa * jnp.exp2(b_mid)).astype(BF16),
            k_dec=(ka * jnp.exp2(b_last - b_mid)).astype(BF16),
            dcol=jnp.transpose(jnp.broadcast_to(jnp.exp2(b_last), (GLA_DK, GLA_DK)))))
    return pre


def _gla_apply(v_ref, items, pre, states):
    c = GLA_CHUNK
    zero_k = jnp.zeros((GLA_DK - c, GLA_DK), BF16)
    zero_v = jnp.zeros((GLA_DK - c, GLA_DV), BF16)
    vs = [v_ref[pl.ds(it['rows'], c), :] for it in items]
    a_list = [lax.dot_general(p['qa'], jnp.concatenate([p['ka'], zero_k], axis=0), NT_DIMS,
                              preferred_element_type=F32) for p in pre]
    kv_list = [lax.dot_general(p['k_dec'], v, TN_DIMS, preferred_element_type=F32) for p, v in zip(pre, vs)]
    states = list(states)
    for it, p, v, a, kv in zip(items, pre, vs, a_list, kv_list):
        s = states[it['dirn']]
        lhs = jnp.concatenate([p['q_in'], jnp.where(it['mask'], a, 0.0).astype(BF16)], axis=1)
        rhs = jnp.concatenate([s.astype(BF16), v, zero_v], axis=0)
        it['out_ref'][pl.ds(it['rows'], c), :] = _dot(lhs, rhs).astype(BF16)
        states[it['dirn']] = s * jnp.concatenate([p['dcol'], p['dcol']], axis=1) + kv
    return states


def _gla_kernel(q_ref, k_ref, v_ref, gf_ref, gb_ref, s0_ref, of_ref, ob_ref, sf_ref, sb_ref, *, n_chunks, group):
    c = GLA_CHUNK
    (tri_f, mask_f), (tri_b, mask_b) = _chunk_consts()
    sf_ref[...] = s0_ref[0]
    sb_ref[...] = jnp.zeros_like(sb_ref)

    def body(i, carry):
        items = []
        for u in range(group):
            items.append(dict(rows=pl.multiple_of((i * group + u) * c, c), g_ref=gf_ref, tri=tri_f, mask=mask_f,
                              mid=c // 2, last=c - 1, dirn=0, out_ref=of_ref))
        for u in range(group):
            items.append(dict(rows=pl.multiple_of((n_chunks - 1 - i * group - u) * c, c), g_ref=gb_ref, tri=tri_b,
                              mask=mask_b, mid=c // 2 - 1, last=0, dirn=1, out_ref=ob_ref))
        pre = _gla_prepare(q_ref, k_ref, items)
        s_f, s_b = _gla_apply(v_ref, items, pre, [sf_ref[...], sb_ref[...]])
        sf_ref[...] = s_f
        sb_ref[...] = s_b
        return carry

    lax.fori_loop(0, n_chunks // group, body, 0)


def _gla(p, s0, batch, seq):
    n_chunks = seq // GLA_CHUNK
    group = _pick_tile(n_chunks, GLA_GROUP)
    kspec = pl.BlockSpec((seq, GLA_DK), lambda b, h: (b, h))
    vspec = pl.BlockSpec((seq, GLA_DV), lambda b, h: (b, h))
    out = jax.ShapeDtypeStruct((batch * seq, GLA_WIDTH), BF16)
    return pl.pallas_call(
        functools.partial(_gla_kernel, n_chunks=n_chunks, group=group),
        grid=(batch, GLA_HEADS),
        in_specs=[kspec, kspec, vspec, kspec, kspec,
                  pl.BlockSpec((1, GLA_DK, GLA_DV), lambda b, h: (h, 0, 0))],
        out_specs=[vspec, vspec],
        out_shape=[out, out],
        scratch_shapes=[pltpu.VMEM((GLA_DK, GLA_DV), F32), pltpu.VMEM((GLA_DK, GLA_DV), F32)],
        compiler_params=pltpu.CompilerParams(dimension_semantics=("arbitrary", "arbitrary"),
                                             vmem_limit_bytes=VMEM_LIMIT),
        name="gla",
    )(p['gq'], p['gk'], p['gv'], p['gf'], p['gb'], s0)


def _gla_meta_state_kernel(k_ref, v_ref, g_ref, s_ref):
    (tri_f, _), _ = _chunk_consts()
    bc = _chunk_cumsum(tri_f, g_ref[...])
    k_dec = (k_ref[...].astype(F32) * jnp.exp2(bc[GLA_CHUNK - 1:GLA_CHUNK] - bc)).astype(BF16)
    s_ref[0] = lax.dot_general(k_dec, v_ref[...], TN_DIMS, preferred_element_type=F32)


def _gla_meta_state(k_pad, v_pad, g_pad):
    return pl.pallas_call(
        _gla_meta_state_kernel,
        grid=(GLA_HEADS,),
        in_specs=[pl.BlockSpec((GLA_CHUNK, GLA_DK), lambda h: (0, h)),
                  pl.BlockSpec((GLA_CHUNK, GLA_DV), lambda h: (0, h)),
                  pl.BlockSpec((GLA_CHUNK, GLA_DK), lambda h: (0, h))],
        out_specs=pl.BlockSpec((1, GLA_DK, GLA_DV), lambda h: (h, 0, 0)),
        out_shape=jax.ShapeDtypeStruct((GLA_HEADS, GLA_DK, GLA_DV), F32),
        name="gla_meta_state",
    )(k_pad, v_pad, g_pad)


def _attn_kernel(qt_ref, k_ref, vt_ref, km_ref, vmt_ref, z_ref, o_ref, acc_ref, m0_ref, s_ref, p_ref, *,
                 tk, n_q, n_kv, unroll):
    n_split = acc_ref.shape[1]
    hw = acc_ref.shape[3]
    kc = tk // hw
    n_tiles = n_q * n_kv

    def stage_a(t, slot):
        qi, j = t // n_kv, t % n_kv
        tile_max = []
        for a in range(n_split):
            k_j = k_ref[0, pl.ds(pl.multiple_of(j * tk, tk), tk), :]
            s = _dot(k_j, qt_ref[0, qi * n_split + a])
            s_ref[slot, a] = s
            tile_max.append(jnp.max(s, axis=0, keepdims=True))
        return tuple(tile_max)

    def stage_b(t, slot, tile_max, ms):
        qi, j = t // n_kv, t % n_kv
        new_ms, alphas = [], []
        for a in range(n_split):
            m_old = jnp.where(j == 0, m0_ref[qi, a], ms[a])
            m_new = jnp.maximum(m_old, tile_max[a])
            alphas.append(jnp.exp2(m_old - m_new))
            p_ref[slot, a] = jnp.exp2(s_ref[slot, a] - m_new).astype(BF16)
            new_ms.append(m_new)
        return tuple(new_ms), tuple(alphas)

    def stage_c(t, slot, alphas):
        qi, j = t // n_kv, t % n_kv
        for a in range(n_split):
            pv = _dot(vt_ref[0, j * kc], p_ref[slot, a, :hw, :])
            for c in range(1, kc):
                pv += _dot(vt_ref[0, j * kc + c], p_ref[slot, a, c * hw:(c + 1) * hw, :])
            acc_ref[qi, a] = alphas[a] * acc_ref[qi, a] + pv

    def step(t, slot, carry):
        ms, alphas, max_next = carry
        ms, new_alphas = stage_b(t + 1, (slot + 1) % ATTN_SLOTS, max_next, ms)
        max_after = stage_a(t + 2, (slot + 2) % ATTN_SLOTS)
        stage_c(t, slot, alphas)
        return ms, new_alphas, max_after

    groups = range(n_q * n_split)
    s_meta = [_dot(km_ref[0], qt_ref[0, g]) for g in groups]
    m_meta = [jnp.max(s, axis=0, keepdims=True) for s in s_meta]
    p_meta = [jnp.exp2(s - m).astype(BF16) for s, m in zip(s_meta, m_meta)]
    for g in groups:
        m0_ref[g // n_split, g % n_split] = m_meta[g]
        acc_ref[g // n_split, g % n_split] = _dot(vmt_ref[0], p_meta[g])

    max_next = stage_a(0, 0)
    ms, alphas = stage_b(0, 0, max_next, tuple(m_meta[:n_split]))
    carry = (ms, alphas, stage_a(1, 1))

    n_steps = n_tiles - 2
    n_loops = n_steps // unroll

    def body(i, carry):
        for u in range(unroll):
            carry = step(i * unroll + u, u % ATTN_SLOTS, carry)
        return carry

    carry = lax.fori_loop(0, n_loops, body, carry)
    for t in range(n_loops * unroll, n_steps):
        carry = step(t, t % ATTN_SLOTS, carry)
    ms, alphas, max_next = carry
    stage_c(n_tiles - 2, (n_tiles - 2) % ATTN_SLOTS, alphas)
    ms, alphas = stage_b(n_tiles - 1, (n_tiles - 1) % ATTN_SLOTS, max_next, ms)
    stage_c(n_tiles - 1, (n_tiles - 1) % ATTN_SLOTS, alphas)

    for g in groups:
        acc = acc_ref[g // n_split, g % n_split]
        o = jnp.transpose(acc[:MLA_D_V] * (1.0 / acc[MLA_D_V:MLA_D_V + 1]))
        rows = slice(g * hw, (g + 1) * hw)
        o_ref[rows, :] = (o * z_ref[rows, :].astype(F32)).astype(BF16)


def _attention(p, meta, batch, seq, tq, tk):
    hw = p['q'].shape[3]
    n_split, n_q, n_kv = tq // hw, seq // tq, seq // tk
    assert tq % hw == 0 and tk % hw == 0 and n_q * n_kv >= 3
    return pl.pallas_call(
        functools.partial(_attn_kernel, tk=tk, n_q=n_q, n_kv=n_kv, unroll=ATTN_UNROLL),
        grid=(batch, MLA_HEADS),
        in_specs=[pl.BlockSpec((1, seq // hw, MLA_QK_PAD, hw), lambda b, h: (h, b, 0, 0)),
                  pl.BlockSpec((1, seq, MLA_QK_PAD), lambda b, h: (h, b, 0)),
                  pl.BlockSpec((1, seq // hw, MLA_V_EXT, hw), lambda b, h: (h, b, 0, 0)),
                  pl.BlockSpec((1, N_META, MLA_QK_PAD), lambda b, h: (h, 0, 0)),
                  pl.BlockSpec((1, MLA_V_EXT, N_META), lambda b, h: (h, 0, 0)),
                  pl.BlockSpec((seq, MLA_D_V), lambda b, h: (b, h))],
        out_specs=pl.BlockSpec((seq, MLA_D_V), lambda b, h: (b, h)),
        out_shape=jax.ShapeDtypeStruct((batch * seq, MLA_WIDTH), BF16),
        scratch_shapes=[pltpu.VMEM((n_q, n_split, MLA_V_EXT, hw), F32),
                        pltpu.VMEM((n_q, n_split, 1, hw), F32),
                        pltpu.VMEM((ATTN_SLOTS, n_split, tk, hw), F32),
                        pltpu.VMEM((ATTN_SLOTS, n_split, tk, hw), BF16)],
        compiler_params=pltpu.CompilerParams(dimension_semantics=("arbitrary", "arbitrary"),
                                             vmem_limit_bytes=VMEM_LIMIT),
        name="mla_attention",
    )(p['q'], p['k'], p['v'], meta['k'], meta['vt'], p['mz'])


def _out_kernel(x_ref, of_ref, ob_ref, gr_ref, om_ref, sa_ref, sb_ref, lng_ref, lnb_ref, gng_ref, woa_ref,
                wob_ref, wout_ref, png_ref, pnb_ref, y_ref):
    h = _layer_norm(x_ref[...], lng_ref[...], lnb_ref[...])
    heads = []
    for hd in range(GLA_HEADS):
        cols = slice(hd * GLA_DV, (hd + 1) * GLA_DV)
        heads.append(_rms_norm(of_ref[:, cols].astype(F32) + ob_ref[:, cols].astype(F32), gng_ref[...]))
    a_in = (jnp.concatenate(heads, axis=1) * gr_ref[...].astype(F32)).astype(BF16)
    branch_a = _dot(a_in, woa_ref[...])
    branch_b = _dot(om_ref[...], wob_ref[...])
    mixed = sa_ref[...].astype(F32) * branch_a + sb_ref[...].astype(F32) * branch_b
    out = _dot(mixed.astype(BF16), wout_ref[...])
    y_ref[...] = _layer_norm(DEEPNORM_ALPHA * h + out, png_ref[...], pnb_ref[...])


def _out_stage(x2d, p, o_gla, o_mla, w, tm):
    n = x2d.shape[0]
    row = pl.BlockSpec((tm, D_MODEL), lambda i: (i, 0))
    consts = [w['emb_g'], w['emb_b'], w['gla_norm_g'], w['w_o_gla'], w['w_o_mla'], w['w_out'], w['post_g'],
              w['post_b']]
    o_fwd, o_bwd = o_gla
    return pl.pallas_call(
        _out_kernel,
        grid=(n // tm,),
        in_specs=[row] * 7 + [_const_spec(c.shape) for c in consts],
        out_specs=row,
        out_shape=jax.ShapeDtypeStruct((n, D_MODEL), F32),
        compiler_params=pltpu.CompilerParams(dimension_semantics=("arbitrary",), vmem_limit_bytes=VMEM_LIMIT),
        name="merge_out",
    )(x2d, o_fwd, o_bwd, p['gr'], o_mla, p['sa'], p['sb'], *consts)


def _rope_tables(start, length):
    inv_freq = 1.0 / (ROPE_THETA ** (jnp.arange(0, MLA_D_ROPE, 2, dtype=F32) / MLA_D_ROPE))
    ang = jnp.arange(start, start + length, dtype=F32)[:, None] * inv_freq[None, :]
    cos, sin = jnp.cos(ang), jnp.sin(ang)
    zero = jnp.zeros_like(cos)
    return (jnp.concatenate([cos, cos, zero, zero], axis=1),
            jnp.concatenate([zero, sin, zero, zero], axis=1),
            jnp.concatenate([-sin, zero, zero, zero], axis=1),
            jnp.concatenate([cos, cos, cos, cos], axis=1),
            jnp.concatenate([zero, sin, zero, sin], axis=1),
            jnp.concatenate([-sin, zero, -sin, zero], axis=1))


def _prepare_weights(emb_ln_g, emb_ln_b, w_in, b_merge, w_gla_gate_f, b_gla_gate_f, w_gla_gate_b, b_gla_gate_b,
                     gla_norm_g, w_o_gla, q_a_norm_g, w_q_b, kv_a_norm_g, w_kv_b, w_o_mla, w_out,
                     post_ln_g, post_ln_b):
    offs = [0]
    for s in IN_SPLITS:
        offs.append(offs[-1] + s)
    col = lambda i: w_in[0][:, offs[i]:offs[i + 1]]
    zcols = lambda n: jnp.zeros((D_MODEL, n), F32)
    w_sm = jnp.concatenate([col(8), col(4), col(5), zcols(LANES - MLA_D_ROPE - 2 * GLA_GATE_RANK),
                            col(6), col(7)], axis=1)
    gate_lo = MLA_D_ROPE
    w_gate = jnp.zeros((LANES, 2 * GLA_KEY_WIDTH), F32)
    w_gate = w_gate.at[gate_lo:gate_lo + GLA_GATE_RANK, :GLA_KEY_WIDTH].set(w_gla_gate_f[0])
    w_gate = w_gate.at[gate_lo + GLA_GATE_RANK:gate_lo + 2 * GLA_GATE_RANK, GLA_KEY_WIDTH:].set(w_gla_gate_b[0])
    wq = w_q_b[0].reshape(MLA_Q_RANK, MLA_HEADS, MLA_D_NOPE + MLA_D_ROPE)
    wq = jnp.concatenate([wq[:, :, :MLA_D_NOPE].reshape(MLA_Q_RANK, -1),
                          wq[:, :, MLA_D_NOPE:].reshape(MLA_Q_RANK, -1)], axis=1)
    wkv = w_kv_b[0].reshape(MLA_KV_RANK, MLA_HEADS, MLA_D_NOPE + MLA_D_V)
    wkv = jnp.concatenate([wkv[:, :, :MLA_D_NOPE].reshape(MLA_KV_RANK, -1),
                           wkv[:, :, MLA_D_NOPE:].reshape(MLA_KV_RANK, -1)], axis=1)
    r2 = lambda a: a.reshape(1, -1).astype(F32)

    def col_tiles(a):
        k, n = a.shape
        return a.astype(BF16).reshape(k, n // MXU_COLS, MXU_COLS).transpose(1, 0, 2)

    return {
        'emb_g': r2(emb_ln_g), 'emb_b': r2(emb_ln_b),
        'w_qk': col_tiles(jnp.concatenate([col(0), col(1)], axis=1)),
        'w_v': col_tiles(col(2)), 'w_r': col_tiles(col(3)), 'w_z': col_tiles(col(9)),
        'w_a': col_tiles(col(10)), 'w_b': col_tiles(col(11)), 'w_sm': w_sm.astype(BF16),
        'b_ma': r2(b_merge[0][:D_MODEL]), 'b_mb': r2(b_merge[0][D_MODEL:]),
        'w_gate': col_tiles(w_gate),
        'b_gate': r2(jnp.concatenate([b_gla_gate_f[0], b_gla_gate_b[0]])),
        'qn_g': r2(q_a_norm_g[0]), 'w_q': col_tiles(wq),
        'kvn_g': r2(kv_a_norm_g[0]), 'w_kv': col_tiles(wkv),
        'gla_norm_g': r2(gla_norm_g[0]),
        'w_o_gla': w_o_gla[0].astype(BF16), 'w_o_mla': w_o_mla[0].astype(BF16), 'w_out': w_out[0].astype(BF16),
        'post_g': r2(post_ln_g[0]), 'post_b': r2(post_ln_b[0]),
    }


def _pick_tile(n, target):
    t = min(n, target)
    while n % t:
        t //= 2
    return t


def _encode(x, w, meta, s0):
    batch, seq, _ = x.shape
    x2d = x.reshape(batch * seq, D_MODEL)
    tm = _pick_tile(seq, INPROJ_ROWS)
    p = _inproj(x2d, tm, seq // tm, w, _rope_tables(N_META, seq), True)
    o_gla = _gla(p, s0, batch, seq)
    o_mla = _attention(p, meta, batch, seq, _pick_tile(seq, ATTN_Q_ROWS), _pick_tile(seq, ATTN_K_ROWS))
    y = _out_stage(x2d, p, o_gla, o_mla, w, _pick_tile(seq, MERGE_ROWS))
    return y.reshape(batch, seq, D_MODEL)


def kernel(x_prompt, x_sample, meta_tokens, emb_ln_g, emb_ln_b, w_in, b_merge, w_gla_gate_f, b_gla_gate_f, w_gla_gate_b, b_gla_gate_b, gla_norm_g, w_o_gla, q_a_norm_g, w_q_b, kv_a_norm_g, w_kv_b, w_o_mla, w_out, post_ln_g, post_ln_b):
    w = _prepare_weights(emb_ln_g, emb_ln_b, w_in, b_merge, w_gla_gate_f, b_gla_gate_f, w_gla_gate_b,
                         b_gla_gate_b, gla_norm_g, w_o_gla, q_a_norm_g, w_q_b, kv_a_norm_g, w_kv_b, w_o_mla,
                         w_out, post_ln_g, post_ln_b)
    meta = _inproj(meta_tokens.astype(F32), N_META, 1, w, _rope_tables(0, N_META), False)
    vt = meta['v'].reshape(N_META, MLA_HEADS, MLA_D_V).transpose(1, 2, 0)
    ones_row = jnp.zeros((MLA_HEADS, MLA_V_EXT - MLA_D_V, N_META), BF16).at[:, 0, :].set(1.0)
    meta['vt'] = jnp.concatenate([vt, ones_row], axis=1)
    lead = ((GLA_CHUNK - N_META, 0), (0, 0))
    s0 = _gla_meta_state(jnp.pad(meta['gk'], lead), jnp.pad(meta['gv'], lead), jnp.pad(meta['gf'], lead))
    y_prompt = _encode(x_prompt, w, meta, s0)
    y_sample = _encode(x_sample, w, meta, s0)
    return (y_prompt, y_sample)
```

```python
import functools

import jax
import jax.numpy as jnp
from jax import lax
from jax.experimental import pallas as pl
from jax.experimental.pallas import tpu as pltpu

D_MODEL = 1024
N_META = 16
GLA_HEADS = 4
GLA_DK = 128
GLA_DV = 256
GLA_KEY_WIDTH = GLA_HEADS * GLA_DK
GLA_WIDTH = GLA_HEADS * GLA_DV
GLA_GATE_RANK = 16
GLA_TAU = 16.0
GLA_CHUNK = 64
GLA_BLOCK = 2 * GLA_CHUNK
MLA_HEADS = 8
MLA_D_NOPE = 128
MLA_D_ROPE = 64
MLA_D_V = 128
MLA_Q_RANK = 384
MLA_KV_RANK = 256
MLA_WIDTH = MLA_HEADS * MLA_D_V
ROPE_THETA = 10000.0
LN_EPS = 1e-5
RMS_EPS = 1e-6
DEPTH = 1
DEEPNORM_ALPHA = (2 * DEPTH) ** 0.25
IN_SPLITS = (GLA_KEY_WIDTH, GLA_KEY_WIDTH, GLA_WIDTH, GLA_WIDTH, GLA_GATE_RANK, GLA_GATE_RANK,
             MLA_Q_RANK, MLA_KV_RANK, MLA_D_ROPE, MLA_WIDTH, D_MODEL, D_MODEL)

LANES = 128
MXU_COLS = 256
MLA_QK_PAD = 2 * LANES
BF16_SUBLANES = 16
MLA_V_EXT = MLA_D_V + BF16_SUBLANES
SMALL_COLS = LANES + MLA_Q_RANK + MLA_KV_RANK
VMEM_LIMIT = 56 * 1024 * 1024

LOG2_E = 1.4426950408889634
INPROJ_ROWS = 256
MERGE_ROWS = 512
ATTN_Q_ROWS = 512
ATTN_K_ROWS = 1024
GLA_GROUP = 16
ATTN_SLOTS = 2
ATTN_UNROLL = 8

F32 = jnp.float32
BF16 = jnp.bfloat16
NT_DIMS = (((1,), (1,)), ((), ()))
TN_DIMS = (((0,), (0,)), ((), ()))


def _dot(a, b):
    return jnp.dot(a, b, preferred_element_type=F32)


def _sigmoid(x):
    return 1.0 / (1.0 + jnp.exp(-x))


def _layer_norm(x, g, b):
    mu = jnp.mean(x, axis=-1, keepdims=True)
    xc = x - mu
    var = jnp.mean(xc * xc, axis=-1, keepdims=True)
    return xc * lax.rsqrt(var + LN_EPS) * g + b


def _rms_norm(x, g):
    ms = jnp.mean(x * x, axis=-1, keepdims=True)
    return x * lax.rsqrt(ms + RMS_EPS) * g


def _rope(blk, c, s1, s2):
    return blk * c + pltpu.roll(blk, 32, 1) * s1 + pltpu.roll(blk, 96, 1) * s2


def _inproj_kernel(x_ref, lng_ref, lnb_ref, wqk_ref, wv_ref, wr_ref, wz_ref, wa_ref, wb_ref, wsm_ref,
                   bma_ref, bmb_ref, wgate_ref, bgate_ref, qng_ref, wq_ref, kvng_ref, wkv_ref,
                   rc_ref, rs1_ref, rs2_ref, rqc_ref, rqs1_ref, rqs2_ref,
                   gq_ref, gk_ref, gv_ref, gr_ref, gf_ref, gb_ref, q_ref, k_ref, v_ref, mz_ref,
                   sa_ref, sb_ref, hb_ref, *, transposed):
    @pl.when(pl.program_id(0) == 0)
    def _():
        hb_ref[...] = jnp.zeros_like(hb_ref)

    def tiles(lhs, w_ref, emit):
        for j in range(w_ref.shape[0]):
            cols = slice(j * MXU_COLS, (j + 1) * MXU_COLS)
            emit(j, cols, _dot(lhs, w_ref[j]))

    def project(w_ref, emit):
        tiles(hb_ref[...], w_ref, emit)

    def split_store(lo_ref, hi_ref, f_lo, f_hi):
        half = lo_ref.shape[1] // MXU_COLS

        def emit(j, cols, t):
            if j < half:
                lo_ref[:, cols] = f_lo(t, cols)
            else:
                hi_ref[:, (j - half) * MXU_COLS:(j - half + 1) * MXU_COLS] = f_hi(t, cols)
        return emit

    def store(ref, f):
        def emit(j, cols, t):
            ref[:, cols] = f(t, cols)
        return emit

    p = _dot(hb_ref[...], wsm_ref[...])
    blk0 = p[:, :LANES]
    gl = blk0.astype(BF16)
    cq = _rms_norm(p[:, LANES:LANES + MLA_Q_RANK], qng_ref[...]).astype(BF16)
    kv_lo = LANES + MLA_Q_RANK
    ckv = _rms_norm(p[:, kv_lo:kv_lo + MLA_KV_RANK], kvng_ref[...]).astype(BF16)

    as_bf16 = lambda t, cols: t.astype(BF16)
    project(wqk_ref, split_store(gq_ref, gk_ref, lambda t, cols: (t * (GLA_DK ** -0.5)).astype(BF16), as_bf16))

    def log_decay(t, cols):
        gx = t + bgate_ref[:, cols]
        softplus2 = jnp.log2(1.0 + jnp.exp2(jnp.abs(gx) * -LOG2_E))
        return (jnp.minimum(gx, 0.0) * LOG2_E - softplus2) * (1.0 / GLA_TAU)
    tiles(gl, wgate_ref, split_store(gf_ref, gb_ref, log_decay, log_decay))

    silu = lambda t, cols: (t * _sigmoid(t)).astype(BF16)
    project(wv_ref, store(gv_ref, as_bf16))
    project(wr_ref, store(gr_ref, silu))

    rc, rs1, rs2 = rc_ref[...], rs1_ref[...], rs2_ref[...]
    scale = (MLA_D_NOPE + MLA_D_ROPE) ** -0.5 * LOG2_E

    tm = x_ref.shape[0]
    nope_tiles = MLA_HEADS * MLA_D_NOPE // MXU_COLS
    rope_end = MLA_D_NOPE + MLA_D_ROPE

    def emit_q(j, cols, t):
        for u in range(MXU_COLS // LANES):
            part = t[:, u * LANES:(u + 1) * LANES]
            if j < nope_tiles:
                q_ref[2 * j + u, 0, :MLA_D_NOPE, :] = (part * scale).T.astype(BF16)
            else:
                pair = (_rope(part, rqc_ref[...], rqs1_ref[...], rqs2_ref[...]) * scale).T.astype(BF16)
                hd = 4 * (j - nope_tiles) + 2 * u
                q_ref[hd, 0, MLA_D_NOPE:rope_end, :] = pair[:MLA_D_ROPE]
                q_ref[hd + 1, 0, MLA_D_NOPE:rope_end, :] = pair[MLA_D_ROPE:]

    if transposed:
        tiles(cq, wq_ref, emit_q)
        q_ref[:, 0, rope_end:, :] = jnp.zeros((MLA_HEADS, MLA_QK_PAD - rope_end, tm), BF16)
    else:
        q_ref[...] = jnp.zeros_like(q_ref)

    kr = _rope(blk0, rc, rs1, rs2).astype(BF16)
    ones_row = (lax.broadcasted_iota(jnp.int32, (MLA_V_EXT - MLA_D_V, tm), 0) == 0).astype(BF16)
    heads_per_tile = MXU_COLS // LANES

    def emit_kv(j, cols, t):
        for u in range(heads_per_tile):
            part = t[:, u * LANES:(u + 1) * LANES]
            hd = (j * heads_per_tile + u) % MLA_HEADS
            if j * heads_per_tile + u < MLA_HEADS:
                k_ref[hd, :, :LANES] = part.astype(BF16)
                k_ref[hd, :, LANES:] = kr
            elif transposed:
                v_ref[hd, 0, :MLA_D_V, :] = part.T.astype(BF16)
                v_ref[hd, 0, MLA_D_V:, :] = ones_row
            else:
                v_ref[:, hd * MLA_D_V:(hd + 1) * MLA_D_V] = part.astype(BF16)
    tiles(ckv, wkv_ref, emit_kv)

    project(wz_ref, store(mz_ref, silu))
    project(wa_ref, store(sa_ref, lambda t, cols: _sigmoid(t + bma_ref[:, cols]).astype(BF16)))
    project(wb_ref, store(sb_ref, lambda t, cols: _sigmoid(t + bmb_ref[:, cols]).astype(BF16)))

    hb_ref[...] = _layer_norm(x_ref[...], lng_ref[...], lnb_ref[...]).astype(BF16)


def _const_spec(shape):
    nd = len(shape)
    return pl.BlockSpec(shape, lambda i: (0,) * nd, pipeline_mode=pl.Buffered(1))


def _inproj(x2d, tm, seq_tiles, w, rope_tabs, transposed):
    n = x2d.shape[0]
    n_tiles = n // tm
    prev = lambda i: jnp.maximum(i - 1, 0)
    row = lambda cols: pl.BlockSpec((tm, cols), lambda i: (prev(i), 0))
    tab = pl.BlockSpec((tm, LANES), lambda i: (prev(i) % seq_tiles, 0))
    consts = [w['emb_g'], w['emb_b'], w['w_qk'], w['w_v'], w['w_r'], w['w_z'], w['w_a'], w['w_b'], w['w_sm'],
              w['b_ma'], w['b_mb'], w['w_gate'], w['b_gate'], w['qn_g'], w['w_q'], w['kvn_g'], w['w_kv']]
    x_spec = pl.BlockSpec((tm, D_MODEL), lambda i: (jnp.minimum(i, n_tiles - 1), 0))
    in_specs = [x_spec] + [_const_spec(c.shape) for c in consts] + [tab] * len(rope_tabs)
    head_spec = pl.BlockSpec((MLA_HEADS, tm, MLA_QK_PAD), lambda i: (0, prev(i), 0))
    head_shape = jax.ShapeDtypeStruct((MLA_HEADS, n, MLA_QK_PAD), BF16)
    if transposed:
        q_spec = pl.BlockSpec((MLA_HEADS, 1, MLA_QK_PAD, tm), lambda i: (0, prev(i), 0, 0))
        q_shape = jax.ShapeDtypeStruct((MLA_HEADS, n_tiles, MLA_QK_PAD, tm), BF16)
        v_spec = pl.BlockSpec((MLA_HEADS, 1, MLA_V_EXT, tm), lambda i: (0, prev(i), 0, 0))
        v_shape = jax.ShapeDtypeStruct((MLA_HEADS, n_tiles, MLA_V_EXT, tm), BF16)
    else:
        q_spec, q_shape = head_spec, head_shape
        v_spec, v_shape = row(MLA_WIDTH), jax.ShapeDtypeStruct((n, MLA_WIDTH), BF16)
    out_shape = [
        jax.ShapeDtypeStruct((n, GLA_KEY_WIDTH), BF16), jax.ShapeDtypeStruct((n, GLA_KEY_WIDTH), BF16),
        jax.ShapeDtypeStruct((n, GLA_WIDTH), BF16), jax.ShapeDtypeStruct((n, GLA_WIDTH), BF16),
        jax.ShapeDtypeStruct((n, GLA_KEY_WIDTH), F32), jax.ShapeDtypeStruct((n, GLA_KEY_WIDTH), F32),
        q_shape, head_shape, v_shape, jax.ShapeDtypeStruct((n, MLA_WIDTH), BF16),
        jax.ShapeDtypeStruct((n, D_MODEL), BF16), jax.ShapeDtypeStruct((n, D_MODEL), BF16),
    ]
    out_specs = [row(GLA_KEY_WIDTH), row(GLA_KEY_WIDTH), row(GLA_WIDTH), row(GLA_WIDTH),
                 row(GLA_KEY_WIDTH), row(GLA_KEY_WIDTH), q_spec, head_spec,
                 v_spec, row(MLA_WIDTH), row(D_MODEL), row(D_MODEL)]
    outs = pl.pallas_call(
        functools.partial(_inproj_kernel, transposed=transposed),
        grid=(n_tiles + 1,),
        in_specs=in_specs,
        out_specs=out_specs,
        out_shape=out_shape,
        scratch_shapes=[pltpu.VMEM((tm, D_MODEL), BF16)],
        compiler_params=pltpu.CompilerParams(dimension_semantics=("arbitrary",), vmem_limit_bytes=VMEM_LIMIT),
        name="inproj",
    )(x2d, *consts, *rope_tabs)
    names = ('gq', 'gk', 'gv', 'gr', 'gf', 'gb', 'q', 'k', 'v', 'mz', 'sa', 'sb')
    return dict(zip(names, outs))


def _chunk_cumsum(tri, g):
    g_hi = g.astype(BF16)
    g_lo = (g - g_hi.astype(F32)).astype(BF16)
    r = _dot(tri, jnp.concatenate([g_hi, g_lo], axis=1))
    return r[:, :GLA_DK] + r[:, GLA_DK:]


def _chunk_consts():
    c = GLA_CHUNK
    r = lax.broadcasted_iota(jnp.int32, (c, c), 0)
    s = lax.broadcasted_iota(jnp.int32, (c, c), 1)
    rm = lax.broadcasted_iota(jnp.int32, (c, GLA_DK), 0)
    sm = lax.broadcasted_iota(jnp.int32, (c, GLA_DK), 1)
    return ((jnp.where(r >= s, 1.0, 0.0).astype(BF16), rm >= sm),
            (jnp.where(r <= s, 1.0, 0.0).astype(BF16), (rm < sm) & (sm < c)))


def _block_consts():
    n, c = GLA_BLOCK, GLA_CHUNK
    r = lax.broadcasted_iota(jnp.int32, (n, n), 0)
    s = lax.broadcasted_iota(jnp.int32, (n, n), 1)
    same_chunk = (r >= c) == (s >= c)
    second = r >= c
    return ((jnp.where(r >= s, 1.0, 0.0).astype(BF16), same_chunk & (r >= s), second),
            (jnp.where(r <= s, 1.0, 0.0).astype(BF16), same_chunk & (r < s), second))


def _gla_prepare(q_ref, k_ref, items):
    n, c = GLA_BLOCK, GLA_CHUNK
    lo, hi = slice(0, c), slice(c, n)
    zero_half = jnp.zeros((c, GLA_DK), BF16)
    bcs = [_chunk_cumsum(it['tri'], it['g_ref'][pl.ds(it['rows'], n), :]) for it in items]
    pre = []
    for it, bc in zip(items, bcs):
        q = q_ref[pl.ds(it['rows'], n), :].astype(F32)
        k = k_ref[pl.ds(it['rows'], n), :].astype(F32)
        mid_a, mid_b, cross, last = (bc[it[name]:it[name] + 1] for name in ('mid_a', 'mid_b', 'cross', 'last'))
        second = it['second']
        ref = jnp.where(second, mid_b, mid_a)
        qa = q * jnp.exp2(bc - ref)
        ka = k * jnp.exp2(ref - bc)
        q_in = qa * jnp.where(second, jnp.exp2(mid_b), jnp.exp2(mid_a))
        k_dec = ka * jnp.where(second, jnp.exp2(last - mid_b), jnp.exp2(last - mid_a))
        if it['dirn'] == 0:
            q2 = jnp.concatenate([zero_half, (q[hi] * jnp.exp2(bc[hi] - cross)).astype(BF16)], axis=0)
            k2 = jnp.concatenate([(k[lo] * jnp.exp2(cross - bc[lo])).astype(BF16), zero_half], axis=0)
        else:
            q2 = jnp.concatenate([(q[lo] * jnp.exp2(bc[lo] - cross)).astype(BF16), zero_half], axis=0)
            k2 = jnp.concatenate([zero_half, (k[hi] * jnp.exp2(cross - bc[hi])).astype(BF16)], axis=0)
        pre.append(dict(
            qs=jnp.concatenate([qa.astype(BF16), q2], axis=0),
            ks=jnp.concatenate([ka.astype(BF16), k2], axis=0),
            q_in=q_in.astype(BF16), k_dec=k_dec.astype(BF16),
            dcol=jnp.transpose(jnp.broadcast_to(jnp.exp2(last), (GLA_DK, GLA_DK)))))
    return pre


def _gla_apply(v_ref, items, pre, states):
    n = GLA_BLOCK
    vs = [v_ref[pl.ds(it['rows'], n), :] for it in items]
    sc_list = [lax.dot_general(p['qs'], p['ks'], NT_DIMS, preferred_element_type=F32) for p in pre]
    kv_list = [lax.dot_general(p['k_dec'], v, TN_DIMS, preferred_element_type=F32) for p, v in zip(pre, vs)]
    states = list(states)
    for it, p, v, sc, kv in zip(items, pre, vs, sc_list, kv_list):
        s = states[it['dirn']]
        a = jnp.where(it['mask'], sc[:n, :n], 0.0) + sc[n:, n:]
        lhs = jnp.concatenate([p['q_in'], a.astype(BF16)], axis=1)
        rhs = jnp.concatenate([s.astype(BF16), v], axis=0)
        it['out_ref'][pl.ds(it['rows'], n), :] = _dot(lhs, rhs).astype(BF16)
        states[it['dirn']] = s * jnp.concatenate([p['dcol'], p['dcol']], axis=1) + kv
    return states


def _gla_kernel(q_ref, k_ref, v_ref, gf_ref, gb_ref, s0_ref, of_ref, ob_ref, sf_ref, sb_ref, *, n_blocks, group):
    n, c = GLA_BLOCK, GLA_CHUNK
    (tri_f, mask_f, second), (tri_b, mask_b, _) = _block_consts()
    sf_ref[...] = s0_ref[0]
    sb_ref[...] = jnp.zeros_like(sb_ref)

    def body(i, carry):
        items = []
        for u in range(group):
            items.append(dict(rows=pl.multiple_of((i * group + u) * n, n), g_ref=gf_ref, tri=tri_f, mask=mask_f,
                              second=second, mid_a=c // 2, mid_b=c + c // 2, cross=c - 1, last=n - 1, dirn=0,
                              out_ref=of_ref))
        for u in range(group):
            items.append(dict(rows=pl.multiple_of((n_blocks - 1 - i * group - u) * n, n), g_ref=gb_ref, tri=tri_b,
                              mask=mask_b, second=second, mid_a=c // 2 - 1, mid_b=c + c // 2 - 1, cross=c, last=0,
                              dirn=1, out_ref=ob_ref))
        pre = _gla_prepare(q_ref, k_ref, items)
        s_f, s_b = _gla_apply(v_ref, items, pre, [sf_ref[...], sb_ref[...]])
        sf_ref[...] = s_f
        sb_ref[...] = s_b
        return carry

    lax.fori_loop(0, n_blocks // group, body, 0)


def _gla(p, s0, batch, seq):
    n_blocks = seq // GLA_BLOCK
    group = _pick_tile(n_blocks, GLA_GROUP)
    kspec = pl.BlockSpec((seq, GLA_DK), lambda b, h: (b, h))
    vspec = pl.BlockSpec((seq, GLA_DV), lambda b, h: (b, h))
    out = jax.ShapeDtypeStruct((batch * seq, GLA_WIDTH), BF16)
    return pl.pallas_call(
        functools.partial(_gla_kernel, n_blocks=n_blocks, group=group),
        grid=(batch, GLA_HEADS),
        in_specs=[kspec, kspec, vspec, kspec, kspec,
                  pl.BlockSpec((1, GLA_DK, GLA_DV), lambda b, h: (h, 0, 0))],
        out_specs=[vspec, vspec],
        out_shape=[out, out],
        scratch_shapes=[pltpu.VMEM((GLA_DK, GLA_DV), F32), pltpu.VMEM((GLA_DK, GLA_DV), F32)],
        compiler_params=pltpu.CompilerParams(dimension_semantics=("arbitrary", "arbitrary"),
                                             vmem_limit_bytes=VMEM_LIMIT),
        name="gla",
    )(p['gq'], p['gk'], p['gv'], p['gf'], p['gb'], s0)


def _gla_meta_state_kernel(k_ref, v_ref, g_ref, s_ref):
    (tri_f, _), _ = _chunk_consts()
    bc = _chunk_cumsum(tri_f, g_ref[...])
    k_dec = (k_ref[...].astype(F32) * jnp.exp2(bc[GLA_CHUNK - 1:GLA_CHUNK] - bc)).astype(BF16)
    s_ref[0] = lax.dot_general(k_dec, v_ref[...], TN_DIMS, preferred_element_type=F32)


def _gla_meta_state(k_pad, v_pad, g_pad):
    return pl.pallas_call(
        _gla_meta_state_kernel,
        grid=(GLA_HEADS,),
        in_specs=[pl.BlockSpec((GLA_CHUNK, GLA_DK), lambda h: (0, h)),
                  pl.BlockSpec((GLA_CHUNK, GLA_DV), lambda h: (0, h)),
                  pl.BlockSpec((GLA_CHUNK, GLA_DK), lambda h: (0, h))],
        out_specs=pl.BlockSpec((1, GLA_DK, GLA_DV), lambda h: (h, 0, 0)),
        out_shape=jax.ShapeDtypeStruct((GLA_HEADS, GLA_DK, GLA_DV), F32),
        name="gla_meta_state",
    )(k_pad, v_pad, g_pad)


def _attn_kernel(qt_ref, k_ref, vt_ref, km_ref, vmt_ref, z_ref, o_ref, acc_ref, m0_ref, s_ref, p_ref, *,
                 tk, n_q, n_kv, unroll):
    n_split = acc_ref.shape[1]
    hw = acc_ref.shape[3]
    kc = tk // hw
    n_tiles = n_q * n_kv

    def stage_a(t, slot):
        qi, j = t // n_kv, t % n_kv
        tile_max = []
        for a in range(n_split):
            k_j = k_ref[0, pl.ds(pl.multiple_of(j * tk, tk), tk), :]
            s = _dot(k_j, qt_ref[0, qi * n_split + a])
            s_ref[slot, a] = s
            tile_max.append(jnp.max(s, axis=0, keepdims=True))
        return tuple(tile_max)

    def stage_b(t, slot, tile_max, ms):
        qi, j = t // n_kv, t % n_kv
        new_ms, alphas = [], []
        for a in range(n_split):
            m_old = jnp.where(j == 0, m0_ref[qi, a], ms[a])
            m_new = jnp.maximum(m_old, tile_max[a])
            alphas.append(jnp.exp2(m_old - m_new))
            p_ref[slot, a] = jnp.exp2(s_ref[slot, a] - m_new).astype(BF16)
            new_ms.append(m_new)
        return tuple(new_ms), tuple(alphas)

    def stage_c(t, slot, alphas):
        qi, j = t // n_kv, t % n_kv
        for a in range(n_split):
            pv = _dot(vt_ref[0, j * kc], p_ref[slot, a, :hw, :])
            for c in range(1, kc):
                pv += _dot(vt_ref[0, j * kc + c], p_ref[slot, a, c * hw:(c + 1) * hw, :])
            acc_ref[qi, a] = alphas[a] * acc_ref[qi, a] + pv

    def step(t, slot, carry):
        ms, alphas, max_next = carry
        ms, new_alphas = stage_b(t + 1, (slot + 1) % ATTN_SLOTS, max_next, ms)
        max_after = stage_a(t + 2, (slot + 2) % ATTN_SLOTS)
        stage_c(t, slot, alphas)
        return ms, new_alphas, max_after

    groups = range(n_q * n_split)
    s_meta = [_dot(km_ref[0], qt_ref[0, g]) for g in groups]
    m_meta = [jnp.max(s, axis=0, keepdims=True) for s in s_meta]
    p_meta = [jnp.exp2(s - m).astype(BF16) for s, m in zip(s_meta, m_meta)]
    for g in groups:
        m0_ref[g // n_split, g % n_split] = m_meta[g]
        acc_ref[g // n_split, g % n_split] = _dot(vmt_ref[0], p_meta[g])

    max_next = stage_a(0, 0)
    ms, alphas = stage_b(0, 0, max_next, tuple(m_meta[:n_split]))
    carry = (ms, alphas, stage_a(1, 1))

    n_steps = n_tiles - 2
    n_loops = n_steps // unroll

    def body(i, carry):
        for u in range(unroll):
            carry = step(i * unroll + u, u % ATTN_SLOTS, carry)
        return carry

    carry = lax.fori_loop(0, n_loops, body, carry)
    for t in range(n_loops * unroll, n_steps):
        carry = step(t, t % ATTN_SLOTS, carry)
    ms, alphas, max_next = carry
    stage_c(n_tiles - 2, (n_tiles - 2) % ATTN_SLOTS, alphas)
    ms, alphas = stage_b(n_tiles - 1, (n_tiles - 1) % ATTN_SLOTS, max_next, ms)
    stage_c(n_tiles - 1, (n_tiles - 1) % ATTN_SLOTS, alphas)

    for g in groups:
        acc = acc_ref[g // n_split, g % n_split]
        o = jnp.transpose(acc[:MLA_D_V] * (1.0 / acc[MLA_D_V:MLA_D_V + 1]))
        rows = slice(g * hw, (g + 1) * hw)
        o_ref[rows, :] = (o * z_ref[rows, :].astype(F32)).astype(BF16)


def _attention(p, meta, batch, seq, tq, tk):
    hw = p['q'].shape[3]
    n_split, n_q, n_kv = tq // hw, seq // tq, seq // tk
    assert tq % hw == 0 and tk % hw == 0 and n_q * n_kv >= 3
    return pl.pallas_call(
        functools.partial(_attn_kernel, tk=tk, n_q=n_q, n_kv=n_kv, unroll=ATTN_UNROLL),
        grid=(batch, MLA_HEADS),
        in_specs=[pl.BlockSpec((1, seq // hw, MLA_QK_PAD, hw), lambda b, h: (h, b, 0, 0)),
                  pl.BlockSpec((1, seq, MLA_QK_PAD), lambda b, h: (h, b, 0)),
                  pl.BlockSpec((1, seq // hw, MLA_V_EXT, hw), lambda b, h: (h, b, 0, 0)),
                  pl.BlockSpec((1, N_META, MLA_QK_PAD), lambda b, h: (h, 0, 0)),
                  pl.BlockSpec((1, MLA_V_EXT, N_META), lambda b, h: (h, 0, 0)),
                  pl.BlockSpec((seq, MLA_D_V), lambda b, h: (b, h))],
        out_specs=pl.BlockSpec((seq, MLA_D_V), lambda b, h: (b, h)),
        out_shape=jax.ShapeDtypeStruct((batch * seq, MLA_WIDTH), BF16),
        scratch_shapes=[pltpu.VMEM((n_q, n_split, MLA_V_EXT, hw), F32),
                        pltpu.VMEM((n_q, n_split, 1, hw), F32),
                        pltpu.VMEM((ATTN_SLOTS, n_split, tk, hw), F32),
                        pltpu.VMEM((ATTN_SLOTS, n_split, tk, hw), BF16)],
        compiler_params=pltpu.CompilerParams(dimension_semantics=("arbitrary", "arbitrary"),
                                             vmem_limit_bytes=VMEM_LIMIT),
        name="mla_attention",
    )(p['q'], p['k'], p['v'], meta['k'], meta['vt'], p['mz'])


def _out_kernel(x_ref, of_ref, ob_ref, gr_ref, om_ref, sa_ref, sb_ref, lng_ref, lnb_ref, gng_ref, woa_ref,
                wob_ref, wout_ref, png_ref, pnb_ref, y_ref):
    h = _layer_norm(x_ref[...], lng_ref[...], lnb_ref[...])
    heads = []
    for hd in range(GLA_HEADS):
        cols = slice(hd * GLA_DV, (hd + 1) * GLA_DV)
        heads.append(_rms_norm(of_ref[:, cols].astype(F32) + ob_ref[:, cols].astype(F32), gng_ref[...]))
    a_in = (jnp.concatenate(heads, axis=1) * gr_ref[...].astype(F32)).astype(BF16)
    branch_a = _dot(a_in, woa_ref[...])
    branch_b = _dot(om_ref[...], wob_ref[...])
    mixed = sa_ref[...].astype(F32) * branch_a + sb_ref[...].astype(F32) * branch_b
    out = _dot(mixed.astype(BF16), wout_ref[...])
    y_ref[...] = _layer_norm(DEEPNORM_ALPHA * h + out, png_ref[...], pnb_ref[...])


def _out_stage(x2d, p, o_gla, o_mla, w, tm):
    n = x2d.shape[0]
    row = pl.BlockSpec((tm, D_MODEL), lambda i: (i, 0))
    consts = [w['emb_g'], w['emb_b'], w['gla_norm_g'], w['w_o_gla'], w['w_o_mla'], w['w_out'], w['post_g'],
              w['post_b']]
    o_fwd, o_bwd = o_gla
    return pl.pallas_call(
        _out_kernel,
        grid=(n // tm,),
        in_specs=[row] * 7 + [_const_spec(c.shape) for c in consts],
        out_specs=row,
        out_shape=jax.ShapeDtypeStruct((n, D_MODEL), F32),
        compiler_params=pltpu.CompilerParams(dimension_semantics=("arbitrary",), vmem_limit_bytes=VMEM_LIMIT),
        name="merge_out",
    )(x2d, o_fwd, o_bwd, p['gr'], o_mla, p['sa'], p['sb'], *consts)


def _rope_tables(start, length):
    inv_freq = 1.0 / (ROPE_THETA ** (jnp.arange(0, MLA_D_ROPE, 2, dtype=F32) / MLA_D_ROPE))
    ang = jnp.arange(start, start + length, dtype=F32)[:, None] * inv_freq[None, :]
    cos, sin = jnp.cos(ang), jnp.sin(ang)
    zero = jnp.zeros_like(cos)
    return (jnp.concatenate([cos, cos, zero, zero], axis=1),
            jnp.concatenate([zero, sin, zero, zero], axis=1),
            jnp.concatenate([-sin, zero, zero, zero], axis=1),
            jnp.concatenate([cos, cos, cos, cos], axis=1),
            jnp.concatenate([zero, sin, zero, sin], axis=1),
            jnp.concatenate([-sin, zero, -sin, zero], axis=1))


def _prepare_weights(emb_ln_g, emb_ln_b, w_in, b_merge, w_gla_gate_f, b_gla_gate_f, w_gla_gate_b, b_gla_gate_b,
                     gla_norm_g, w_o_gla, q_a_norm_g, w_q_b, kv_a_norm_g, w_kv_b, w_o_mla, w_out,
                     post_ln_g, post_ln_b):
    offs = [0]
    for s in IN_SPLITS:
        offs.append(offs[-1] + s)
    col = lambda i: w_in[0][:, offs[i]:offs[i + 1]]
    zcols = lambda n: jnp.zeros((D_MODEL, n), F32)
    w_sm = jnp.concatenate([col(8), col(4), col(5), zcols(LANES - MLA_D_ROPE - 2 * GLA_GATE_RANK),
                            col(6), col(7)], axis=1)
    gate_lo = MLA_D_ROPE
    w_gate = jnp.zeros((LANES, 2 * GLA_KEY_WIDTH), F32)
    w_gate = w_gate.at[gate_lo:gate_lo + GLA_GATE_RANK, :GLA_KEY_WIDTH].set(w_gla_gate_f[0])
    w_gate = w_gate.at[gate_lo + GLA_GATE_RANK:gate_lo + 2 * GLA_GATE_RANK, GLA_KEY_WIDTH:].set(w_gla_gate_b[0])
    wq = w_q_b[0].reshape(MLA_Q_RANK, MLA_HEADS, MLA_D_NOPE + MLA_D_ROPE)
    wq = jnp.concatenate([wq[:, :, :MLA_D_NOPE].reshape(MLA_Q_RANK, -1),
                          wq[:, :, MLA_D_NOPE:].reshape(MLA_Q_RANK, -1)], axis=1)
    wkv = w_kv_b[0].reshape(MLA_KV_RANK, MLA_HEADS, MLA_D_NOPE + MLA_D_V)
    wkv = jnp.concatenate([wkv[:, :, :MLA_D_NOPE].reshape(MLA_KV_RANK, -1),
                           wkv[:, :, MLA_D_NOPE:].reshape(MLA_KV_RANK, -1)], axis=1)
    r2 = lambda a: a.reshape(1, -1).astype(F32)

    def col_tiles(a):
        k, n = a.shape
        return a.astype(BF16).reshape(k, n // MXU_COLS, MXU_COLS).transpose(1, 0, 2)

    return {
        'emb_g': r2(emb_ln_g), 'emb_b': r2(emb_ln_b),
        'w_qk': col_tiles(jnp.concatenate([col(0), col(1)], axis=1)),
        'w_v': col_tiles(col(2)), 'w_r': col_tiles(col(3)), 'w_z': col_tiles(col(9)),
        'w_a': col_tiles(col(10)), 'w_b': col_tiles(col(11)), 'w_sm': w_sm.astype(BF16),
        'b_ma': r2(b_merge[0][:D_MODEL]), 'b_mb': r2(b_merge[0][D_MODEL:]),
        'w_gate': col_tiles(w_gate),
        'b_gate': r2(jnp.concatenate([b_gla_gate_f[0], b_gla_gate_b[0]])),
        'qn_g': r2(q_a_norm_g[0]), 'w_q': col_tiles(wq),
        'kvn_g': r2(kv_a_norm_g[0]), 'w_kv': col_tiles(wkv),
        'gla_norm_g': r2(gla_norm_g[0]),
        'w_o_gla': w_o_gla[0].astype(BF16), 'w_o_mla': w_o_mla[0].astype(BF16), 'w_out': w_out[0].astype(BF16),
        'post_g': r2(post_ln_g[0]), 'post_b': r2(post_ln_b[0]),
    }


def _pick_tile(n, target):
    t = min(n, target)
    while n % t:
        t //= 2
    return t


def _encode(x, w, meta, s0):
    batch, seq, _ = x.shape
    x2d = x.reshape(batch * seq, D_MODEL)
    tm = _pick_tile(seq, INPROJ_ROWS)
    p = _inproj(x2d, tm, seq // tm, w, _rope_tables(N_META, seq), True)
    o_gla = _gla(p, s0, batch, seq)
    o_mla = _attention(p, meta, batch, seq, _pick_tile(seq, ATTN_Q_ROWS), _pick_tile(seq, ATTN_K_ROWS))
    y = _out_stage(x2d, p, o_gla, o_mla, w, _pick_tile(seq, MERGE_ROWS))
    return y.reshape(batch, seq, D_MODEL)


def kernel(x_prompt, x_sample, meta_tokens, emb_ln_g, emb_ln_b, w_in, b_merge, w_gla_gate_f, b_gla_gate_f, w_gla_gate_b, b_gla_gate_b, gla_norm_g, w_o_gla, q_a_norm_g, w_q_b, kv_a_norm_g, w_kv_b, w_o_mla, w_out, post_ln_g, post_ln_b):
    w = _prepare_weights(emb_ln_g, emb_ln_b, w_in, b_merge, w_gla_gate_f, b_gla_gate_f, w_gla_gate_b,
                         b_gla_gate_b, gla_norm_g, w_o_gla, q_a_norm_g, w_q_b, kv_a_norm_g, w_kv_b, w_o_mla,
                         w_out, post_ln_g, post_ln_b)
    meta = _inproj(meta_tokens.astype(F32), N_META, 1, w, _rope_tables(0, N_META), False)
    vt = meta['v'].reshape(N_META, MLA_HEADS, MLA_D_V).transpose(1, 2, 0)
    ones_row = jnp.zeros((MLA_HEADS, MLA_V_EXT - MLA_D_V, N_META), BF16).at[:, 0, :].set(1.0)
    meta['vt'] = jnp.concatenate([vt, ones_row], axis=1)
    lead = ((GLA_CHUNK - N_META, 0), (0, 0))
    s0 = _gla_meta_state(jnp.pad(meta['gk'], lead), jnp.pad(meta['gv'], lead), jnp.pad(meta['gf'], lead))
    y_prompt = _encode(x_prompt, w, meta, s0)
    y_sample = _encode(x_sample, w, meta, s0)
    return (y_prompt, y_sample)
```

```python
import functools

import jax
import jax.numpy as jnp
from jax import lax
from jax.experimental import pallas as pl
from jax.experimental.pallas import tpu as pltpu

D_MODEL = 1024
N_META = 16
GLA_HEADS = 4
GLA_DK = 128
GLA_DV = 256
GLA_KEY_WIDTH = GLA_HEADS * GLA_DK
GLA_WIDTH = GLA_HEADS * GLA_DV
GLA_GATE_RANK = 16
GLA_TAU = 16.0
GLA_CHUNK = 64
GLA_BLOCK = 2 * GLA_CHUNK
MLA_HEADS = 8
MLA_D_NOPE = 128
MLA_D_ROPE = 64
MLA_D_V = 128
MLA_Q_RANK = 384
MLA_KV_RANK = 256
MLA_WIDTH = MLA_HEADS * MLA_D_V
ROPE_THETA = 10000.0
LN_EPS = 1e-5
RMS_EPS = 1e-6
DEPTH = 1
DEEPNORM_ALPHA = (2 * DEPTH) ** 0.25
IN_SPLITS = (GLA_KEY_WIDTH, GLA_KEY_WIDTH, GLA_WIDTH, GLA_WIDTH, GLA_GATE_RANK, GLA_GATE_RANK,
             MLA_Q_RANK, MLA_KV_RANK, MLA_D_ROPE, MLA_WIDTH, D_MODEL, D_MODEL)

LANES = 128
MXU_COLS = 256
MLA_QK_PAD = 2 * LANES
BF16_SUBLANES = 16
MLA_V_EXT = MLA_D_V + BF16_SUBLANES
SMALL_COLS = LANES + MLA_Q_RANK + MLA_KV_RANK
VMEM_LIMIT = 56 * 1024 * 1024

LOG2_E = 1.4426950408889634
INPROJ_ROWS = 256
MERGE_ROWS = 512
ATTN_Q_ROWS = 512
ATTN_K_ROWS = 1024
GLA_GROUP = 16
ATTN_SLOTS = 2
ATTN_UNROLL = 8

F32 = jnp.float32
BF16 = jnp.bfloat16
NT_DIMS = (((1,), (1,)), ((), ()))
TN_DIMS = (((0,), (0,)), ((), ()))


def _dot(a, b):
    return jnp.dot(a, b, preferred_element_type=F32)


def _sigmoid(x):
    return 0.5 * jnp.tanh(0.5 * x) + 0.5


def _layer_norm(x, g, b):
    mu = jnp.mean(x, axis=-1, keepdims=True)
    xc = x - mu
    var = jnp.mean(xc * xc, axis=-1, keepdims=True)
    return xc * lax.rsqrt(var + LN_EPS) * g + b


def _rms_norm(x, g):
    ms = jnp.mean(x * x, axis=-1, keepdims=True)
    return x * lax.rsqrt(ms + RMS_EPS) * g


def _rope(blk, c, s1, s2):
    return blk * c + pltpu.roll(blk, 32, 1) * s1 + pltpu.roll(blk, 96, 1) * s2


def _inproj_kernel(x_ref, lng_ref, lnb_ref, wqk_ref, wv_ref, wr_ref, wz_ref, wa_ref, wb_ref, wsm_ref,
                   bma_ref, bmb_ref, wgate_ref, bgate_ref, qng_ref, wq_ref, kvng_ref, wkv_ref,
                   rc_ref, rs1_ref, rs2_ref, rqc_ref, rqs1_ref, rqs2_ref,
                   gq_ref, gk_ref, gv_ref, gr_ref, gf_ref, gb_ref, q_ref, k_ref, v_ref, mz_ref,
                   sa_ref, sb_ref, hb_ref, *, transposed):
    @pl.when(pl.program_id(0) == 0)
    def _():
        hb_ref[...] = jnp.zeros_like(hb_ref)

    def tiles(lhs, w_ref, emit):
        for j in range(w_ref.shape[0]):
            cols = slice(j * MXU_COLS, (j + 1) * MXU_COLS)
            emit(j, cols, _dot(lhs, w_ref[j]))

    def project(w_ref, emit):
        tiles(hb_ref[...], w_ref, emit)

    def split_store(lo_ref, hi_ref, f_lo, f_hi):
        half = lo_ref.shape[1] // MXU_COLS

        def emit(j, cols, t):
            if j < half:
                lo_ref[:, cols] = f_lo(t, cols)
            else:
                hi_ref[:, (j - half) * MXU_COLS:(j - half + 1) * MXU_COLS] = f_hi(t, cols)
        return emit

    def store(ref, f):
        def emit(j, cols, t):
            ref[:, cols] = f(t, cols)
        return emit

    p = _dot(hb_ref[...], wsm_ref[...])
    blk0 = p[:, :LANES]
    gl = blk0.astype(BF16)
    cq = _rms_norm(p[:, LANES:LANES + MLA_Q_RANK], qng_ref[...]).astype(BF16)
    kv_lo = LANES + MLA_Q_RANK
    ckv = _rms_norm(p[:, kv_lo:kv_lo + MLA_KV_RANK], kvng_ref[...]).astype(BF16)

    as_bf16 = lambda t, cols: t.astype(BF16)
    project(wqk_ref, split_store(gq_ref, gk_ref, lambda t, cols: (t * (GLA_DK ** -0.5)).astype(BF16), as_bf16))

    def log_decay(t, cols):
        gx = t + bgate_ref[:, cols]
        softplus2 = jnp.log2(1.0 + jnp.exp2(jnp.abs(gx) * -LOG2_E))
        return (jnp.minimum(gx, 0.0) * LOG2_E - softplus2) * (1.0 / GLA_TAU)
    tiles(gl, wgate_ref, split_store(gf_ref, gb_ref, log_decay, log_decay))

    silu = lambda t, cols: (t * _sigmoid(t)).astype(BF16)
    project(wv_ref, store(gv_ref, as_bf16))
    project(wr_ref, store(gr_ref, silu))

    rc, rs1, rs2 = rc_ref[...], rs1_ref[...], rs2_ref[...]
    scale = (MLA_D_NOPE + MLA_D_ROPE) ** -0.5 * LOG2_E

    tm = x_ref.shape[0]
    nope_tiles = MLA_HEADS * MLA_D_NOPE // MXU_COLS
    rope_end = MLA_D_NOPE + MLA_D_ROPE

    def emit_q(j, cols, t):
        for u in range(MXU_COLS // LANES):
            part = t[:, u * LANES:(u + 1) * LANES]
            if j < nope_tiles:
                q_ref[2 * j + u, 0, :MLA_D_NOPE, :] = (part * scale).T.astype(BF16)
            else:
                pair = (_rope(part, rqc_ref[...], rqs1_ref[...], rqs2_ref[...]) * scale).T.astype(BF16)
                hd = 4 * (j - nope_tiles) + 2 * u
                q_ref[hd, 0, MLA_D_NOPE:rope_end, :] = pair[:MLA_D_ROPE]
                q_ref[hd + 1, 0, MLA_D_NOPE:rope_end, :] = pair[MLA_D_ROPE:]

    if transposed:
        tiles(cq, wq_ref, emit_q)
        q_ref[:, 0, rope_end:, :] = jnp.zeros((MLA_HEADS, MLA_QK_PAD - rope_end, tm), BF16)
    else:
        q_ref[...] = jnp.zeros_like(q_ref)

    kr = _rope(blk0, rc, rs1, rs2).astype(BF16)
    ones_row = (lax.broadcasted_iota(jnp.int32, (MLA_V_EXT - MLA_D_V, tm), 0) == 0).astype(BF16)
    heads_per_tile = MXU_COLS // LANES

    def emit_kv(j, cols, t):
        for u in range(heads_per_tile):
            part = t[:, u * LANES:(u + 1) * LANES]
            hd = (j * heads_per_tile + u) % MLA_HEADS
            if j * heads_per_tile + u < MLA_HEADS:
                k_ref[hd, :, :LANES] = part.astype(BF16)
                k_ref[hd, :, LANES:] = kr
            elif transposed:
                v_ref[hd, 0, :MLA_D_V, :] = part.T.astype(BF16)
                v_ref[hd, 0, MLA_D_V:, :] = ones_row
            else:
                v_ref[:, hd * MLA_D_V:(hd + 1) * MLA_D_V] = part.astype(BF16)
    tiles(ckv, wkv_ref, emit_kv)

    project(wz_ref, store(mz_ref, silu))
    project(wa_ref, store(sa_ref, lambda t, cols: _sigmoid(t + bma_ref[:, cols]).astype(BF16)))
    project(wb_ref, store(sb_ref, lambda t, cols: _sigmoid(t + bmb_ref[:, cols]).astype(BF16)))

    hb_ref[...] = _layer_norm(x_ref[...], lng_ref[...], lnb_ref[...]).astype(BF16)


def _const_spec(shape):
    nd = len(shape)
    return pl.BlockSpec(shape, lambda i: (0,) * nd, pipeline_mode=pl.Buffered(1))


def _inproj(x2d, tm, seq_tiles, w, rope_tabs, transposed):
    n = x2d.shape[0]
    n_tiles = n // tm
    prev = lambda i: jnp.maximum(i - 1, 0)
    row = lambda cols: pl.BlockSpec((tm, cols), lambda i: (prev(i), 0))
    tab = pl.BlockSpec((tm, LANES), lambda i: (prev(i) % seq_tiles, 0))
    consts = [w['emb_g'], w['emb_b'], w['w_qk'], w['w_v'], w['w_r'], w['w_z'], w['w_a'], w['w_b'], w['w_sm'],
              w['b_ma'], w['b_mb'], w['w_gate'], w['b_gate'], w['qn_g'], w['w_q'], w['kvn_g'], w['w_kv']]
    x_spec = pl.BlockSpec((tm, D_MODEL), lambda i: (jnp.minimum(i, n_tiles - 1), 0))
    in_specs = [x_spec] + [_const_spec(c.shape) for c in consts] + [tab] * len(rope_tabs)
    head_spec = pl.BlockSpec((MLA_HEADS, tm, MLA_QK_PAD), lambda i: (0, prev(i), 0))
    head_shape = jax.ShapeDtypeStruct((MLA_HEADS, n, MLA_QK_PAD), BF16)
    if transposed:
        q_spec = pl.BlockSpec((MLA_HEADS, 1, MLA_QK_PAD, tm), lambda i: (0, prev(i), 0, 0))
        q_shape = jax.ShapeDtypeStruct((MLA_HEADS, n_tiles, MLA_QK_PAD, tm), BF16)
        v_spec = pl.BlockSpec((MLA_HEADS, 1, MLA_V_EXT, tm), lambda i: (0, prev(i), 0, 0))
        v_shape = jax.ShapeDtypeStruct((MLA_HEADS, n_tiles, MLA_V_EXT, tm), BF16)
    else:
        q_spec, q_shape = head_spec, head_shape
        v_spec, v_shape = row(MLA_WIDTH), jax.ShapeDtypeStruct((n, MLA_WIDTH), BF16)
    out_shape = [
        jax.ShapeDtypeStruct((n, GLA_KEY_WIDTH), BF16), jax.ShapeDtypeStruct((n, GLA_KEY_WIDTH), BF16),
        jax.ShapeDtypeStruct((n, GLA_WIDTH), BF16), jax.ShapeDtypeStruct((n, GLA_WIDTH), BF16),
        jax.ShapeDtypeStruct((n, GLA_KEY_WIDTH), F32), jax.ShapeDtypeStruct((n, GLA_KEY_WIDTH), F32),
        q_shape, head_shape, v_shape, jax.ShapeDtypeStruct((n, MLA_WIDTH), BF16),
        jax.ShapeDtypeStruct((n, D_MODEL), BF16), jax.ShapeDtypeStruct((n, D_MODEL), BF16),
    ]
    out_specs = [row(GLA_KEY_WIDTH), row(GLA_KEY_WIDTH), row(GLA_WIDTH), row(GLA_WIDTH),
                 row(GLA_KEY_WIDTH), row(GLA_KEY_WIDTH), q_spec, head_spec,
                 v_spec, row(MLA_WIDTH), row(D_MODEL), row(D_MODEL)]
    outs = pl.pallas_call(
        functools.partial(_inproj_kernel, transposed=transposed),
        grid=(n_tiles + 1,),
        in_specs=in_specs,
        out_specs=out_specs,
        out_shape=out_shape,
        scratch_shapes=[pltpu.VMEM((tm, D_MODEL), BF16)],
        compiler_params=pltpu.CompilerParams(dimension_semantics=("arbitrary",), vmem_limit_bytes=VMEM_LIMIT),
        name="inproj",
    )(x2d, *consts, *rope_tabs)
    names = ('gq', 'gk', 'gv', 'gr', 'gf', 'gb', 'q', 'k', 'v', 'mz', 'sa', 'sb')
    return dict(zip(names, outs))


def _chunk_cumsum(tri, g):
    g_hi = g.astype(BF16)
    g_lo = (g - g_hi.astype(F32)).astype(BF16)
    r = _dot(tri, jnp.concatenate([g_hi, g_lo], axis=1))
    return r[:, :GLA_DK] + r[:, GLA_DK:]


def _block_consts():
    n, c = GLA_BLOCK, GLA_CHUNK
    r = lax.broadcasted_iota(jnp.int32, (n, n), 0)
    s = lax.broadcasted_iota(jnp.int32, (n, n), 1)
    same_chunk = (r >= c) == (s >= c)
    second = r >= c
    return ((jnp.where(r >= s, 1.0, 0.0).astype(BF16), same_chunk & (r >= s), second),
            (jnp.where(r <= s, 1.0, 0.0).astype(BF16), same_chunk & (r < s), second))


def _gla_prepare(q_ref, k_ref, items):
    n, c = GLA_BLOCK, GLA_CHUNK
    lo, hi = slice(0, c), slice(c, n)
    zero_half = jnp.zeros((c, GLA_DK), BF16)
    bcs = [_chunk_cumsum(it['tri'], it['g_ref'][pl.ds(it['rows'], n), :]) for it in items]
    pre = []
    for it, bc in zip(items, bcs):
        q = q_ref[pl.ds(it['rows'], n), :].astype(F32)
        k = k_ref[pl.ds(it['rows'], n), :].astype(F32)
        mid_a, mid_b, cross, last = (bc[it[name]:it[name] + 1] for name in ('mid_a', 'mid_b', 'cross', 'last'))
        second = it['second']
        ref = jnp.where(second, mid_b, mid_a)
        qa = q * jnp.exp2(bc - ref)
        ka = k * jnp.exp2(ref - bc)
        q_in = qa * jnp.where(second, jnp.exp2(mid_b), jnp.exp2(mid_a))
        k_dec = ka * jnp.where(second, jnp.exp2(last - mid_b), jnp.exp2(last - mid_a))
        if it['dirn'] == 0:
            q2 = jnp.concatenate([zero_half, (q[hi] * jnp.exp2(bc[hi] - cross)).astype(BF16)], axis=0)
            k2 = jnp.concatenate([(k[lo] * jnp.exp2(cross - bc[lo])).astype(BF16), zero_half], axis=0)
        else:
            q2 = jnp.concatenate([(q[lo] * jnp.exp2(bc[lo] - cross)).astype(BF16), zero_half], axis=0)
            k2 = jnp.concatenate([zero_half, (k[hi] * jnp.exp2(cross - bc[hi])).astype(BF16)], axis=0)
        pre.append(dict(
            qs=jnp.concatenate([qa.astype(BF16), q2], axis=0),
            ks=jnp.concatenate([ka.astype(BF16), k2], axis=0),
            q_in=q_in.astype(BF16), k_dec=k_dec.astype(BF16),
            dcol=jnp.transpose(jnp.broadcast_to(jnp.exp2(last), (GLA_DK, GLA_DK)))))
    return pre


def _gla_apply(v_ref, items, pre, states):
    n = GLA_BLOCK
    vs = [v_ref[pl.ds(it['rows'], n), :] for it in items]
    sc_list = [lax.dot_general(p['qs'], p['ks'], NT_DIMS, preferred_element_type=F32) for p in pre]
    kv_list = [lax.dot_general(p['k_dec'], v, TN_DIMS, preferred_element_type=F32) for p, v in zip(pre, vs)]
    states = list(states)
    for it, p, v, sc, kv in zip(items, pre, vs, sc_list, kv_list):
        s = states[it['dirn']]
        a = jnp.where(it['mask'], sc[:n, :n], 0.0) + sc[n:, n:]
        lhs = jnp.concatenate([p['q_in'], a.astype(BF16)], axis=1)
        rhs = jnp.concatenate([s.astype(BF16), v], axis=0)
        it['out_ref'][pl.ds(it['rows'], n), :] = _dot(lhs, rhs).astype(BF16)
        states[it['dirn']] = s * jnp.concatenate([p['dcol'], p['dcol']], axis=1) + kv
    return states


def _gla_kernel(q_ref, k_ref, v_ref, gf_ref, gb_ref, s0_ref, of_ref, ob_ref, sf_ref, sb_ref, *, n_blocks, group):
    n, c = GLA_BLOCK, GLA_CHUNK
    (tri_f, mask_f, second), (tri_b, mask_b, _) = _block_consts()
    sf_ref[...] = s0_ref[0]
    sb_ref[...] = jnp.zeros_like(sb_ref)

    def body(i, carry):
        items = []
        for u in range(group):
            items.append(dict(rows=pl.multiple_of((i * group + u) * n, n), g_ref=gf_ref, tri=tri_f, mask=mask_f,
                              second=second, mid_a=c // 2, mid_b=c + c // 2, cross=c - 1, last=n - 1, dirn=0,
                              out_ref=of_ref))
        for u in range(group):
            items.append(dict(rows=pl.multiple_of((n_blocks - 1 - i * group - u) * n, n), g_ref=gb_ref, tri=tri_b,
                              mask=mask_b, second=second, mid_a=c // 2 - 1, mid_b=c + c // 2 - 1, cross=c, last=0,
                              dirn=1, out_ref=ob_ref))
        pre = _gla_prepare(q_ref, k_ref, items)
        s_f, s_b = _gla_apply(v_ref, items, pre, [sf_ref[...], sb_ref[...]])
        sf_ref[...] = s_f
        sb_ref[...] = s_b
        return carry

    lax.fori_loop(0, n_blocks // group, body, 0)


def _gla(p, s0, batch, seq):
    n_blocks = seq // GLA_BLOCK
    group = _pick_tile(n_blocks, GLA_GROUP)
    kspec = pl.BlockSpec((seq, GLA_DK), lambda b, h: (b, h))
    vspec = pl.BlockSpec((seq, GLA_DV), lambda b, h: (b, h))
    out = jax.ShapeDtypeStruct((batch * seq, GLA_WIDTH), BF16)
    return pl.pallas_call(
        functools.partial(_gla_kernel, n_blocks=n_blocks, group=group),
        grid=(batch, GLA_HEADS),
        in_specs=[kspec, kspec, vspec, kspec, kspec,
                  pl.BlockSpec((1, GLA_DK, GLA_DV), lambda b, h: (h, 0, 0))],
        out_specs=[vspec, vspec],
        out_shape=[out, out],
        scratch_shapes=[pltpu.VMEM((GLA_DK, GLA_DV), F32), pltpu.VMEM((GLA_DK, GLA_DV), F32)],
        compiler_params=pltpu.CompilerParams(dimension_semantics=("arbitrary", "arbitrary"),
                                             vmem_limit_bytes=VMEM_LIMIT),
        name="gla",
    )(p['gq'], p['gk'], p['gv'], p['gf'], p['gb'], s0)


def _gla_meta_state_kernel(k_ref, v_ref, g_ref, s_ref):
    c = GLA_CHUNK
    lower = lax.broadcasted_iota(jnp.int32, (c, c), 0) >= lax.broadcasted_iota(jnp.int32, (c, c), 1)
    bc = _chunk_cumsum(jnp.where(lower, 1.0, 0.0).astype(BF16), g_ref[...])
    k_dec = (k_ref[...].astype(F32) * jnp.exp2(bc[GLA_CHUNK - 1:GLA_CHUNK] - bc)).astype(BF16)
    s_ref[0] = lax.dot_general(k_dec, v_ref[...], TN_DIMS, preferred_element_type=F32)


def _gla_meta_state(k_pad, v_pad, g_pad):
    return pl.pallas_call(
        _gla_meta_state_kernel,
        grid=(GLA_HEADS,),
        in_specs=[pl.BlockSpec((GLA_CHUNK, GLA_DK), lambda h: (0, h)),
                  pl.BlockSpec((GLA_CHUNK, GLA_DV), lambda h: (0, h)),
                  pl.BlockSpec((GLA_CHUNK, GLA_DK), lambda h: (0, h))],
        out_specs=pl.BlockSpec((1, GLA_DK, GLA_DV), lambda h: (h, 0, 0)),
        out_shape=jax.ShapeDtypeStruct((GLA_HEADS, GLA_DK, GLA_DV), F32),
        name="gla_meta_state",
    )(k_pad, v_pad, g_pad)


def _attn_kernel(qt_ref, k_ref, vt_ref, km_ref, vmt_ref, z_ref, o_ref, acc_ref, m0_ref, s_ref, p_ref, *,
                 tk, n_q, n_kv, unroll):
    n_split = acc_ref.shape[1]
    hw = acc_ref.shape[3]
    kc = tk // hw
    n_tiles = n_q * n_kv

    def stage_a(t, slot):
        qi, j = t // n_kv, t % n_kv
        tile_max = []
        for a in range(n_split):
            k_j = k_ref[0, pl.ds(pl.multiple_of(j * tk, tk), tk), :]
            s = _dot(k_j, qt_ref[0, qi * n_split + a])
            s_ref[slot, a] = s
            tile_max.append(jnp.max(s, axis=0, keepdims=True))
        return tuple(tile_max)

    def stage_b(t, slot, tile_max, ms):
        qi, j = t // n_kv, t % n_kv
        new_ms, alphas = [], []
        for a in range(n_split):
            m_old = jnp.where(j == 0, m0_ref[qi, a], ms[a])
            m_new = jnp.maximum(m_old, tile_max[a])
            alphas.append(jnp.exp2(m_old - m_new))
            p_ref[slot, a] = jnp.exp2(s_ref[slot, a] - m_new).astype(BF16)
            new_ms.append(m_new)
        return tuple(new_ms), tuple(alphas)

    def stage_c(t, slot, alphas):
        qi, j = t // n_kv, t % n_kv
        for a in range(n_split):
            pv = _dot(vt_ref[0, j * kc], p_ref[slot, a, :hw, :])
            for c in range(1, kc):
                pv += _dot(vt_ref[0, j * kc + c], p_ref[slot, a, c * hw:(c + 1) * hw, :])
            acc_ref[qi, a] = alphas[a] * acc_ref[qi, a] + pv

    def step(t, slot, carry):
        ms, alphas, max_next = carry
        ms, new_alphas = stage_b(t + 1, (slot + 1) % ATTN_SLOTS, max_next, ms)
        max_after = stage_a(t + 2, (slot + 2) % ATTN_SLOTS)
        stage_c(t, slot, alphas)
        return ms, new_alphas, max_after

    groups = range(n_q * n_split)
    s_meta = [_dot(km_ref[0], qt_ref[0, g]) for g in groups]
    m_meta = [jnp.max(s, axis=0, keepdims=True) for s in s_meta]
    p_meta = [jnp.exp2(s - m).astype(BF16) for s, m in zip(s_meta, m_meta)]
    for g in groups:
        m0_ref[g // n_split, g % n_split] = m_meta[g]
        acc_ref[g // n_split, g % n_split] = _dot(vmt_ref[0], p_meta[g])

    max_next = stage_a(0, 0)
    ms, alphas = stage_b(0, 0, max_next, tuple(m_meta[:n_split]))
    carry = (ms, alphas, stage_a(1, 1))

    n_steps = n_tiles - 2
    n_loops = n_steps // unroll

    def body(i, carry):
        for u in range(unroll):
            carry = step(i * unroll + u, u % ATTN_SLOTS, carry)
        return carry

    carry = lax.fori_loop(0, n_loops, body, carry)
    for t in range(n_loops * unroll, n_steps):
        carry = step(t, t % ATTN_SLOTS, carry)
    ms, alphas, max_next = carry
    stage_c(n_tiles - 2, (n_tiles - 2) % ATTN_SLOTS, alphas)
    ms, alphas = stage_b(n_tiles - 1, (n_tiles - 1) % ATTN_SLOTS, max_next, ms)
    stage_c(n_tiles - 1, (n_tiles - 1) % ATTN_SLOTS, alphas)

    for g in groups:
        acc = acc_ref[g // n_split, g % n_split]
        o = jnp.transpose(acc[:MLA_D_V] * (1.0 / acc[MLA_D_V:MLA_D_V + 1]))
        rows = slice(g * hw, (g + 1) * hw)
        o_ref[rows, :] = (o * z_ref[rows, :].astype(F32)).astype(BF16)


def _attention(p, meta, batch, seq, tq, tk):
    hw = p['q'].shape[3]
    n_split, n_q, n_kv = tq // hw, seq // tq, seq // tk
    assert tq % hw == 0 and tk % hw == 0 and n_q * n_kv >= 3
    return pl.pallas_call(
        functools.partial(_attn_kernel, tk=tk, n_q=n_q, n_kv=n_kv, unroll=ATTN_UNROLL),
        grid=(batch, MLA_HEADS),
        in_specs=[pl.BlockSpec((1, seq // hw, MLA_QK_PAD, hw), lambda b, h: (h, b, 0, 0)),
                  pl.BlockSpec((1, seq, MLA_QK_PAD), lambda b, h: (h, b, 0)),
                  pl.BlockSpec((1, seq // hw, MLA_V_EXT, hw), lambda b, h: (h, b, 0, 0)),
                  pl.BlockSpec((1, N_META, MLA_QK_PAD), lambda b, h: (h, 0, 0)),
                  pl.BlockSpec((1, MLA_V_EXT, N_META), lambda b, h: (h, 0, 0)),
                  pl.BlockSpec((seq, MLA_D_V), lambda b, h: (b, h))],
        out_specs=pl.BlockSpec((seq, MLA_D_V), lambda b, h: (b, h)),
        out_shape=jax.ShapeDtypeStruct((batch * seq, MLA_WIDTH), BF16),
        scratch_shapes=[pltpu.VMEM((n_q, n_split, MLA_V_EXT, hw), F32),
                        pltpu.VMEM((n_q, n_split, 1, hw), F32),
                        pltpu.VMEM((ATTN_SLOTS, n_split, tk, hw), F32),
                        pltpu.VMEM((ATTN_SLOTS, n_split, tk, hw), BF16)],
        compiler_params=pltpu.CompilerParams(dimension_semantics=("arbitrary", "arbitrary"),
                                             vmem_limit_bytes=VMEM_LIMIT),
        name="mla_attention",
    )(p['q'], p['k'], p['v'], meta['k'], meta['vt'], p['mz'])


def _out_kernel(x_ref, of_ref, ob_ref, gr_ref, om_ref, sa_ref, sb_ref, lng_ref, lnb_ref, gng_ref, woa_ref,
                wob_ref, wout_ref, png_ref, pnb_ref, y_ref):
    h = _layer_norm(x_ref[...], lng_ref[...], lnb_ref[...])
    heads = []
    for hd in range(GLA_HEADS):
        cols = slice(hd * GLA_DV, (hd + 1) * GLA_DV)
        heads.append(_rms_norm(of_ref[:, cols].astype(F32) + ob_ref[:, cols].astype(F32), gng_ref[...]))
    a_in = (jnp.concatenate(heads, axis=1) * gr_ref[...].astype(F32)).astype(BF16)
    branch_a = _dot(a_in, woa_ref[...])
    branch_b = _dot(om_ref[...], wob_ref[...])
    mixed = sa_ref[...].astype(F32) * branch_a + sb_ref[...].astype(F32) * branch_b
    out = _dot(mixed.astype(BF16), wout_ref[...])
    y_ref[...] = _layer_norm(DEEPNORM_ALPHA * h + out, png_ref[...], pnb_ref[...])


def _out_stage(x2d, p, o_gla, o_mla, w, tm):
    n = x2d.shape[0]
    row = pl.BlockSpec((tm, D_MODEL), lambda i: (i, 0))
    consts = [w['emb_g'], w['emb_b'], w['gla_norm_g'], w['w_o_gla'], w['w_o_mla'], w['w_out'], w['post_g'],
              w['post_b']]
    o_fwd, o_bwd = o_gla
    return pl.pallas_call(
        _out_kernel,
        grid=(n // tm,),
        in_specs=[row] * 7 + [_const_spec(c.shape) for c in consts],
        out_specs=row,
        out_shape=jax.ShapeDtypeStruct((n, D_MODEL), F32),
        compiler_params=pltpu.CompilerParams(dimension_semantics=("arbitrary",), vmem_limit_bytes=VMEM_LIMIT),
        name="merge_out",
    )(x2d, o_fwd, o_bwd, p['gr'], o_mla, p['sa'], p['sb'], *consts)


def _rope_tables(start, length):
    inv_freq = 1.0 / (ROPE_THETA ** (jnp.arange(0, MLA_D_ROPE, 2, dtype=F32) / MLA_D_ROPE))
    ang = jnp.arange(start, start + length, dtype=F32)[:, None] * inv_freq[None, :]
    cos, sin = jnp.cos(ang), jnp.sin(ang)
    zero = jnp.zeros_like(cos)
    return (jnp.concatenate([cos, cos, zero, zero], axis=1),
            jnp.concatenate([zero, sin, zero, zero], axis=1),
            jnp.concatenate([-sin, zero, zero, zero], axis=1),
            jnp.concatenate([cos, cos, cos, cos], axis=1),
            jnp.concatenate([zero, sin, zero, sin], axis=1),
            jnp.concatenate([-sin, zero, -sin, zero], axis=1))


def _prepare_weights(emb_ln_g, emb_ln_b, w_in, b_merge, w_gla_gate_f, b_gla_gate_f, w_gla_gate_b, b_gla_gate_b,
                     gla_norm_g, w_o_gla, q_a_norm_g, w_q_b, kv_a_norm_g, w_kv_b, w_o_mla, w_out,
                     post_ln_g, post_ln_b):
    offs = [0]
    for s in IN_SPLITS:
        offs.append(offs[-1] + s)
    col = lambda i: w_in[0][:, offs[i]:offs[i + 1]]
    zcols = lambda n: jnp.zeros((D_MODEL, n), F32)
    w_sm = jnp.concatenate([col(8), col(4), col(5), zcols(LANES - MLA_D_ROPE - 2 * GLA_GATE_RANK),
                            col(6), col(7)], axis=1)
    gate_lo = MLA_D_ROPE
    w_gate = jnp.zeros((LANES, 2 * GLA_KEY_WIDTH), F32)
    w_gate = w_gate.at[gate_lo:gate_lo + GLA_GATE_RANK, :GLA_KEY_WIDTH].set(w_gla_gate_f[0])
    w_gate = w_gate.at[gate_lo + GLA_GATE_RANK:gate_lo + 2 * GLA_GATE_RANK, GLA_KEY_WIDTH:].set(w_gla_gate_b[0])
    wq = w_q_b[0].reshape(MLA_Q_RANK, MLA_HEADS, MLA_D_NOPE + MLA_D_ROPE)
    wq = jnp.concatenate([wq[:, :, :MLA_D_NOPE].reshape(MLA_Q_RANK, -1),
                          wq[:, :, MLA_D_NOPE:].reshape(MLA_Q_RANK, -1)], axis=1)
    wkv = w_kv_b[0].reshape(MLA_KV_RANK, MLA_HEADS, MLA_D_NOPE + MLA_D_V)
    wkv = jnp.concatenate([wkv[:, :, :MLA_D_NOPE].reshape(MLA_KV_RANK, -1),
                           wkv[:, :, MLA_D_NOPE:].reshape(MLA_KV_RANK, -1)], axis=1)
    r2 = lambda a: a.reshape(1, -1).astype(F32)

    def col_tiles(a):
        k, n = a.shape
        return a.astype(BF16).reshape(k, n // MXU_COLS, MXU_COLS).transpose(1, 0, 2)

    return {
        'emb_g': r2(emb_ln_g), 'emb_b': r2(emb_ln_b),
        'w_qk': col_tiles(jnp.concatenate([col(0), col(1)], axis=1)),
        'w_v': col_tiles(col(2)), 'w_r': col_tiles(col(3)), 'w_z': col_tiles(col(9)),
        'w_a': col_tiles(col(10)), 'w_b': col_tiles(col(11)), 'w_sm': w_sm.astype(BF16),
        'b_ma': r2(b_merge[0][:D_MODEL]), 'b_mb': r2(b_merge[0][D_MODEL:]),
        'w_gate': col_tiles(w_gate),
        'b_gate': r2(jnp.concatenate([b_gla_gate_f[0], b_gla_gate_b[0]])),
        'qn_g': r2(q_a_norm_g[0]), 'w_q': col_tiles(wq),
        'kvn_g': r2(kv_a_norm_g[0]), 'w_kv': col_tiles(wkv),
        'gla_norm_g': r2(gla_norm_g[0]),
        'w_o_gla': w_o_gla[0].astype(BF16), 'w_o_mla': w_o_mla[0].astype(BF16), 'w_out': w_out[0].astype(BF16),
        'post_g': r2(post_ln_g[0]), 'post_b': r2(post_ln_b[0]),
    }


def _pick_tile(n, target):
    t = min(n, target)
    while n % t:
        t //= 2
    return t


def _encode(x, w, meta, s0):
    batch, seq, _ = x.shape
    x2d = x.reshape(batch * seq, D_MODEL)
    tm = _pick_tile(seq, INPROJ_ROWS)
    p = _inproj(x2d, tm, seq // tm, w, _rope_tables(N_META, seq), True)
    o_gla = _gla(p, s0, batch, seq)
    o_mla = _attention(p, meta, batch, seq, _pick_tile(seq, ATTN_Q_ROWS), _pick_tile(seq, ATTN_K_ROWS))
    y = _out_stage(x2d, p, o_gla, o_mla, w, _pick_tile(seq, MERGE_ROWS))
    return y.reshape(batch, seq, D_MODEL)


def kernel(x_prompt, x_sample, meta_tokens, emb_ln_g, emb_ln_b, w_in, b_merge, w_gla_gate_f, b_gla_gate_f, w_gla_gate_b, b_gla_gate_b, gla_norm_g, w_o_gla, q_a_norm_g, w_q_b, kv_a_norm_g, w_kv_b, w_o_mla, w_out, post_ln_g, post_ln_b):
    w = _prepare_weights(emb_ln_g, emb_ln_b, w_in, b_merge, w_gla_gate_f, b_gla_gate_f, w_gla_gate_b,
                         b_gla_gate_b, gla_norm_g, w_o_gla, q_a_norm_g, w_q_b, kv_a_norm_g, w_kv_b, w_o_mla,
                         w_out, post_ln_g, post_ln_b)
    meta = _inproj(meta_tokens.astype(F32), N_META, 1, w, _rope_tables(0, N_META), False)
    vt = meta['v'].reshape(N_META, MLA_HEADS, MLA_D_V).transpose(1, 2, 0)
    ones_row = jnp.zeros((MLA_HEADS, MLA_V_EXT - MLA_D_V, N_META), BF16).at[:, 0, :].set(1.0)
    meta['vt'] = jnp.concatenate([vt, ones_row], axis=1)
    lead = ((GLA_CHUNK - N_META, 0), (0, 0))
    s0 = _gla_meta_state(jnp.pad(meta['gk'], lead), jnp.pad(meta['gv'], lead), jnp.pad(meta['gf'], lead))
    y_prompt = _encode(x_prompt, w, meta, s0)
    y_sample = _encode(x_sample, w, meta, s0)
    return (y_prompt, y_sample)
```

```python
import functools

import jax
import jax.numpy as jnp
from jax import lax
from jax.experimental import pallas as pl
from jax.experimental.pallas import tpu as pltpu

D_MODEL = 1024
N_META = 16
GLA_HEADS = 4
GLA_DK = 128
GLA_DV = 256
GLA_KEY_WIDTH = GLA_HEADS * GLA_DK
GLA_WIDTH = GLA_HEADS * GLA_DV
GLA_GATE_RANK = 16
GLA_TAU = 16.0
GLA_CHUNK = 64
GLA_BLOCK = 2 * GLA_CHUNK
MLA_HEADS = 8
MLA_D_NOPE = 128
MLA_D_ROPE = 64
MLA_D_V = 128
MLA_Q_RANK = 384
MLA_KV_RANK = 256
MLA_WIDTH = MLA_HEADS * MLA_D_V
ROPE_THETA = 10000.0
LN_EPS = 1e-5
RMS_EPS = 1e-6
DEPTH = 1
DEEPNORM_ALPHA = (2 * DEPTH) ** 0.25
IN_SPLITS = (GLA_KEY_WIDTH, GLA_KEY_WIDTH, GLA_WIDTH, GLA_WIDTH, GLA_GATE_RANK, GLA_GATE_RANK,
             MLA_Q_RANK, MLA_KV_RANK, MLA_D_ROPE, MLA_WIDTH, D_MODEL, D_MODEL)

LANES = 128
MXU_COLS = 256
MLA_QK_PAD = 2 * LANES
BF16_SUBLANES = 16
MLA_V_EXT = MLA_D_V + BF16_SUBLANES
SMALL_COLS = LANES + MLA_Q_RANK + MLA_KV_RANK
VMEM_LIMIT = 56 * 1024 * 1024

LOG2_E = 1.4426950408889634
INPROJ_ROWS = 256
MERGE_ROWS = 512
ATTN_Q_ROWS = 512
ATTN_K_ROWS = 1024
GLA_GROUP = 16
ATTN_SLOTS = 2
ATTN_UNROLL = 10

F32 = jnp.float32
BF16 = jnp.bfloat16
NT_DIMS = (((1,), (1,)), ((), ()))
TN_DIMS = (((0,), (0,)), ((), ()))


def _dot(a, b):
    return jnp.dot(a, b, preferred_element_type=F32)


def _sigmoid(x):
    return 0.5 * jnp.tanh(0.5 * x) + 0.5


def _layer_norm(x, g, b):
    mu = jnp.mean(x, axis=-1, keepdims=True)
    xc = x - mu
    var = jnp.mean(xc * xc, axis=-1, keepdims=True)
    return xc * lax.rsqrt(var + LN_EPS) * g + b


def _rms_norm(x, g):
    ms = jnp.mean(x * x, axis=-1, keepdims=True)
    return x * lax.rsqrt(ms + RMS_EPS) * g


def _rope(blk, c, s1, s2):
    return blk * c + pltpu.roll(blk, 32, 1) * s1 + pltpu.roll(blk, 96, 1) * s2


def _inproj_kernel(x_ref, lng_ref, lnb_ref, wqk_ref, wv_ref, wr_ref, wz_ref, wa_ref, wb_ref, wsm_ref,
                   bma_ref, bmb_ref, wgate_ref, bgate_ref, qng_ref, wq_ref, kvng_ref, wkv_ref,
                   rc_ref, rs1_ref, rs2_ref, rqc_ref, rqs1_ref, rqs2_ref,
                   gq_ref, gk_ref, gv_ref, gr_ref, gf_ref, gb_ref, q_ref, k_ref, v_ref, mz_ref,
                   sa_ref, sb_ref, hb_ref, *, transposed):
    @pl.when(pl.program_id(0) == 0)
    def _():
        hb_ref[...] = jnp.zeros_like(hb_ref)

    def tiles(lhs, w_ref, emit):
        for j in range(w_ref.shape[0]):
            cols = slice(j * MXU_COLS, (j + 1) * MXU_COLS)
            emit(j, cols, _dot(lhs, w_ref[j]))

    def project(w_ref, emit):
        tiles(hb_ref[...], w_ref, emit)

    def split_store(lo_ref, hi_ref, f_lo, f_hi):
        half = lo_ref.shape[1] // MXU_COLS

        def emit(j, cols, t):
            if j < half:
                lo_ref[:, cols] = f_lo(t, cols)
            else:
                hi_ref[:, (j - half) * MXU_COLS:(j - half + 1) * MXU_COLS] = f_hi(t, cols)
        return emit

    def store(ref, f):
        def emit(j, cols, t):
            ref[:, cols] = f(t, cols)
        return emit

    p = _dot(hb_ref[...], wsm_ref[...])
    blk0 = p[:, :LANES]
    gl = blk0.astype(BF16)
    cq = _rms_norm(p[:, LANES:LANES + MLA_Q_RANK], qng_ref[...]).astype(BF16)
    kv_lo = LANES + MLA_Q_RANK
    ckv = _rms_norm(p[:, kv_lo:kv_lo + MLA_KV_RANK], kvng_ref[...]).astype(BF16)

    as_bf16 = lambda t, cols: t.astype(BF16)
    project(wqk_ref, split_store(gq_ref, gk_ref, lambda t, cols: (t * (GLA_DK ** -0.5)).astype(BF16), as_bf16))

    def log_decay(t, cols):
        gx = t + bgate_ref[:, cols]
        softplus2 = jnp.log2(1.0 + jnp.exp2(jnp.abs(gx) * -LOG2_E))
        return (jnp.minimum(gx, 0.0) * LOG2_E - softplus2) * (1.0 / GLA_TAU)
    tiles(gl, wgate_ref, split_store(gf_ref, gb_ref, log_decay, log_decay))

    silu = lambda t, cols: (t * _sigmoid(t)).astype(BF16)
    project(wv_ref, store(gv_ref, as_bf16))
    project(wr_ref, store(gr_ref, silu))

    rc, rs1, rs2 = rc_ref[...], rs1_ref[...], rs2_ref[...]
    scale = (MLA_D_NOPE + MLA_D_ROPE) ** -0.5 * LOG2_E

    tm = x_ref.shape[0]
    nope_tiles = MLA_HEADS * MLA_D_NOPE // MXU_COLS
    rope_end = MLA_D_NOPE + MLA_D_ROPE

    def emit_q(j, cols, t):
        for u in range(MXU_COLS // LANES):
            part = t[:, u * LANES:(u + 1) * LANES]
            if j < nope_tiles:
                q_ref[2 * j + u, 0, :MLA_D_NOPE, :] = (part * scale).T.astype(BF16)
            else:
                pair = (_rope(part, rqc_ref[...], rqs1_ref[...], rqs2_ref[...]) * scale).T.astype(BF16)
                hd = 4 * (j - nope_tiles) + 2 * u
                q_ref[hd, 0, MLA_D_NOPE:rope_end, :] = pair[:MLA_D_ROPE]
                q_ref[hd + 1, 0, MLA_D_NOPE:rope_end, :] = pair[MLA_D_ROPE:]

    if transposed:
        tiles(cq, wq_ref, emit_q)
        q_ref[:, 0, rope_end:, :] = jnp.zeros((MLA_HEADS, MLA_QK_PAD - rope_end, tm), BF16)
    else:
        q_ref[...] = jnp.zeros_like(q_ref)

    kr = _rope(blk0, rc, rs1, rs2).astype(BF16)
    ones_row = (lax.broadcasted_iota(jnp.int32, (MLA_V_EXT - MLA_D_V, tm), 0) == 0).astype(BF16)
    heads_per_tile = MXU_COLS // LANES

    def emit_kv(j, cols, t):
        for u in range(heads_per_tile):
            part = t[:, u * LANES:(u + 1) * LANES]
            hd = (j * heads_per_tile + u) % MLA_HEADS
            if j * heads_per_tile + u < MLA_HEADS:
                k_ref[hd, :, :LANES] = part.astype(BF16)
                k_ref[hd, :, LANES:] = kr
            elif transposed:
                v_ref[hd, 0, :MLA_D_V, :] = part.T.astype(BF16)
                v_ref[hd, 0, MLA_D_V:, :] = ones_row
            else:
                v_ref[:, hd * MLA_D_V:(hd + 1) * MLA_D_V] = part.astype(BF16)
    tiles(ckv, wkv_ref, emit_kv)

    project(wz_ref, store(mz_ref, silu))
    project(wa_ref, store(sa_ref, lambda t, cols: _sigmoid(t + bma_ref[:, cols]).astype(BF16)))
    project(wb_ref, store(sb_ref, lambda t, cols: _sigmoid(t + bmb_ref[:, cols]).astype(BF16)))

    hb_ref[...] = _layer_norm(x_ref[...], lng_ref[...], lnb_ref[...]).astype(BF16)


def _const_spec(shape):
    nd = len(shape)
    return pl.BlockSpec(shape, lambda i: (0,) * nd, pipeline_mode=pl.Buffered(1))


def _inproj(x2d, tm, seq_tiles, w, rope_tabs, transposed):
    n = x2d.shape[0]
    n_tiles = n // tm
    prev = lambda i: jnp.maximum(i - 1, 0)
    row = lambda cols: pl.BlockSpec((tm, cols), lambda i: (prev(i), 0))
    tab = pl.BlockSpec((tm, LANES), lambda i: (prev(i) % seq_tiles, 0))
    consts = [w['emb_g'], w['emb_b'], w['w_qk'], w['w_v'], w['w_r'], w['w_z'], w['w_a'], w['w_b'], w['w_sm'],
              w['b_ma'], w['b_mb'], w['w_gate'], w['b_gate'], w['qn_g'], w['w_q'], w['kvn_g'], w['w_kv']]
    x_spec = pl.BlockSpec((tm, D_MODEL), lambda i: (jnp.minimum(i, n_tiles - 1), 0))
    in_specs = [x_spec] + [_const_spec(c.shape) for c in consts] + [tab] * len(rope_tabs)
    head_spec = pl.BlockSpec((MLA_HEADS, tm, MLA_QK_PAD), lambda i: (0, prev(i), 0))
    head_shape = jax.ShapeDtypeStruct((MLA_HEADS, n, MLA_QK_PAD), BF16)
    if transposed:
        q_spec = pl.BlockSpec((MLA_HEADS, 1, MLA_QK_PAD, tm), lambda i: (0, prev(i), 0, 0))
        q_shape = jax.ShapeDtypeStruct((MLA_HEADS, n_tiles, MLA_QK_PAD, tm), BF16)
        v_spec = pl.BlockSpec((MLA_HEADS, 1, MLA_V_EXT, tm), lambda i: (0, prev(i), 0, 0))
        v_shape = jax.ShapeDtypeStruct((MLA_HEADS, n_tiles, MLA_V_EXT, tm), BF16)
    else:
        q_spec, q_shape = head_spec, head_shape
        v_spec, v_shape = row(MLA_WIDTH), jax.ShapeDtypeStruct((n, MLA_WIDTH), BF16)
    out_shape = [
        jax.ShapeDtypeStruct((n, GLA_KEY_WIDTH), BF16), jax.ShapeDtypeStruct((n, GLA_KEY_WIDTH), BF16),
        jax.ShapeDtypeStruct((n, GLA_WIDTH), BF16), jax.ShapeDtypeStruct((n, GLA_WIDTH), BF16),
        jax.ShapeDtypeStruct((n, GLA_KEY_WIDTH), F32), jax.ShapeDtypeStruct((n, GLA_KEY_WIDTH), F32),
        q_shape, head_shape, v_shape, jax.ShapeDtypeStruct((n, MLA_WIDTH), BF16),
        jax.ShapeDtypeStruct((n, D_MODEL), BF16), jax.ShapeDtypeStruct((n, D_MODEL), BF16),
    ]
    out_specs = [row(GLA_KEY_WIDTH), row(GLA_KEY_WIDTH), row(GLA_WIDTH), row(GLA_WIDTH),
                 row(GLA_KEY_WIDTH), row(GLA_KEY_WIDTH), q_spec, head_spec,
                 v_spec, row(MLA_WIDTH), row(D_MODEL), row(D_MODEL)]
    outs = pl.pallas_call(
        functools.partial(_inproj_kernel, transposed=transposed),
        grid=(n_tiles + 1,),
        in_specs=in_specs,
        out_specs=out_specs,
        out_shape=out_shape,
        scratch_shapes=[pltpu.VMEM((tm, D_MODEL), BF16)],
        compiler_params=pltpu.CompilerParams(dimension_semantics=("arbitrary",), vmem_limit_bytes=VMEM_LIMIT),
        name="inproj",
    )(x2d, *consts, *rope_tabs)
    names = ('gq', 'gk', 'gv', 'gr', 'gf', 'gb', 'q', 'k', 'v', 'mz', 'sa', 'sb')
    return dict(zip(names, outs))


def _chunk_cumsum(tri, g):
    g_hi = g.astype(BF16)
    g_lo = (g - g_hi.astype(F32)).astype(BF16)
    r = _dot(tri, jnp.concatenate([g_hi, g_lo], axis=1))
    return r[:, :GLA_DK] + r[:, GLA_DK:]


def _block_consts():
    n, c = GLA_BLOCK, GLA_CHUNK
    r = lax.broadcasted_iota(jnp.int32, (n, n), 0)
    s = lax.broadcasted_iota(jnp.int32, (n, n), 1)
    same_chunk = (r >= c) == (s >= c)
    second = r >= c
    return ((jnp.where(r >= s, 1.0, 0.0).astype(BF16), same_chunk & (r >= s), second),
            (jnp.where(r <= s, 1.0, 0.0).astype(BF16), same_chunk & (r < s), second))


def _gla_prepare(q_ref, k_ref, items):
    n, c = GLA_BLOCK, GLA_CHUNK
    lo, hi = slice(0, c), slice(c, n)
    zero_half = jnp.zeros((c, GLA_DK), BF16)
    bcs = [_chunk_cumsum(it['tri'], it['g_ref'][pl.ds(it['rows'], n), :]) for it in items]
    pre = []
    for it, bc in zip(items, bcs):
        q = q_ref[pl.ds(it['rows'], n), :].astype(F32)
        k = k_ref[pl.ds(it['rows'], n), :].astype(F32)
        mid_a, mid_b, cross, last = (bc[it[name]:it[name] + 1] for name in ('mid_a', 'mid_b', 'cross', 'last'))
        second = it['second']
        ref = jnp.where(second, mid_b, mid_a)
        qa = q * jnp.exp2(bc - ref)
        ka = k * jnp.exp2(ref - bc)
        q_in = qa * jnp.where(second, jnp.exp2(mid_b), jnp.exp2(mid_a))
        k_dec = ka * jnp.where(second, jnp.exp2(last - mid_b), jnp.exp2(last - mid_a))
        if it['dirn'] == 0:
            q2 = jnp.concatenate([zero_half, (q[hi] * jnp.exp2(bc[hi] - cross)).astype(BF16)], axis=0)
            k2 = jnp.concatenate([(k[lo] * jnp.exp2(cross - bc[lo])).astype(BF16), zero_half], axis=0)
        else:
            q2 = jnp.concatenate([(q[lo] * jnp.exp2(bc[lo] - cross)).astype(BF16), zero_half], axis=0)
            k2 = jnp.concatenate([zero_half, (k[hi] * jnp.exp2(cross - bc[hi])).astype(BF16)], axis=0)
        pre.append(dict(
            qs=jnp.concatenate([qa.astype(BF16), q2], axis=0),
            ks=jnp.concatenate([ka.astype(BF16), k2], axis=0),
            q_in=q_in.astype(BF16), k_dec=k_dec.astype(BF16),
            dcol=jnp.transpose(jnp.broadcast_to(jnp.exp2(last), (GLA_DK, GLA_DK)))))
    return pre


def _gla_apply(v_ref, items, pre, states):
    n = GLA_BLOCK
    vs = [v_ref[pl.ds(it['rows'], n), :] for it in items]
    sc_list = [lax.dot_general(p['qs'], p['ks'], NT_DIMS, preferred_element_type=F32) for p in pre]
    kv_list = [lax.dot_general(p['k_dec'], v, TN_DIMS, preferred_element_type=F32) for p, v in zip(pre, vs)]
    states = list(states)
    for it, p, v, sc, kv in zip(items, pre, vs, sc_list, kv_list):
        s = states[it['dirn']]
        a = jnp.where(it['mask'], sc[:n, :n], 0.0) + sc[n:, n:]
        lhs = jnp.concatenate([p['q_in'], a.astype(BF16)], axis=1)
        rhs = jnp.concatenate([s.astype(BF16), v], axis=0)
        it['out_ref'][pl.ds(it['rows'], n), :] = _dot(lhs, rhs).astype(BF16)
        states[it['dirn']] = s * jnp.concatenate([p['dcol'], p['dcol']], axis=1) + kv
    return states


def _gla_kernel(q_ref, k_ref, v_ref, gf_ref, gb_ref, s0_ref, of_ref, ob_ref, sf_ref, sb_ref, *, n_blocks, group):
    n, c = GLA_BLOCK, GLA_CHUNK
    (tri_f, mask_f, second), (tri_b, mask_b, _) = _block_consts()
    sf_ref[...] = s0_ref[0]
    sb_ref[...] = jnp.zeros_like(sb_ref)

    def body(i, carry):
        items = []
        for u in range(group):
            items.append(dict(rows=pl.multiple_of((i * group + u) * n, n), g_ref=gf_ref, tri=tri_f, mask=mask_f,
                              second=second, mid_a=c // 2, mid_b=c + c // 2, cross=c - 1, last=n - 1, dirn=0,
                              out_ref=of_ref))
        for u in range(group):
            items.append(dict(rows=pl.multiple_of((n_blocks - 1 - i * group - u) * n, n), g_ref=gb_ref, tri=tri_b,
                              mask=mask_b, second=second, mid_a=c // 2 - 1, mid_b=c + c // 2 - 1, cross=c, last=0,
                              dirn=1, out_ref=ob_ref))
        pre = _gla_prepare(q_ref, k_ref, items)
        s_f, s_b = _gla_apply(v_ref, items, pre, [sf_ref[...], sb_ref[...]])
        sf_ref[...] = s_f
        sb_ref[...] = s_b
        return carry

    lax.fori_loop(0, n_blocks // group, body, 0)


def _gla(p, s0, batch, seq):
    n_blocks = seq // GLA_BLOCK
    group = _pick_tile(n_blocks, GLA_GROUP)
    kspec = pl.BlockSpec((seq, GLA_DK), lambda b, h: (b, h))
    vspec = pl.BlockSpec((seq, GLA_DV), lambda b, h: (b, h))
    out = jax.ShapeDtypeStruct((batch * seq, GLA_WIDTH), BF16)
    return pl.pallas_call(
        functools.partial(_gla_kernel, n_blocks=n_blocks, group=group),
        grid=(batch, GLA_HEADS),
        in_specs=[kspec, kspec, vspec, kspec, kspec,
                  pl.BlockSpec((1, GLA_DK, GLA_DV), lambda b, h: (h, 0, 0))],
        out_specs=[vspec, vspec],
        out_shape=[out, out],
        scratch_shapes=[pltpu.VMEM((GLA_DK, GLA_DV), F32), pltpu.VMEM((GLA_DK, GLA_DV), F32)],
        compiler_params=pltpu.CompilerParams(dimension_semantics=("arbitrary", "arbitrary"),
                                             vmem_limit_bytes=VMEM_LIMIT),
        name="gla",
    )(p['gq'], p['gk'], p['gv'], p['gf'], p['gb'], s0)


def _gla_meta_state_kernel(k_ref, v_ref, g_ref, s_ref):
    c = GLA_CHUNK
    lower = lax.broadcasted_iota(jnp.int32, (c, c), 0) >= lax.broadcasted_iota(jnp.int32, (c, c), 1)
    bc = _chunk_cumsum(jnp.where(lower, 1.0, 0.0).astype(BF16), g_ref[...])
    k_dec = (k_ref[...].astype(F32) * jnp.exp2(bc[GLA_CHUNK - 1:GLA_CHUNK] - bc)).astype(BF16)
    s_ref[0] = lax.dot_general(k_dec, v_ref[...], TN_DIMS, preferred_element_type=F32)


def _gla_meta_state(k_pad, v_pad, g_pad):
    return pl.pallas_call(
        _gla_meta_state_kernel,
        grid=(GLA_HEADS,),
        in_specs=[pl.BlockSpec((GLA_CHUNK, GLA_DK), lambda h: (0, h)),
                  pl.BlockSpec((GLA_CHUNK, GLA_DV), lambda h: (0, h)),
                  pl.BlockSpec((GLA_CHUNK, GLA_DK), lambda h: (0, h))],
        out_specs=pl.BlockSpec((1, GLA_DK, GLA_DV), lambda h: (h, 0, 0)),
        out_shape=jax.ShapeDtypeStruct((GLA_HEADS, GLA_DK, GLA_DV), F32),
        name="gla_meta_state",
    )(k_pad, v_pad, g_pad)


def _attn_kernel(qt_ref, k_ref, vt_ref, km_ref, vmt_ref, z_ref, o_ref, acc_ref, m0_ref, s_ref, p_ref, *,
                 tk, n_q, n_kv, unroll):
    n_split = acc_ref.shape[1]
    hw = acc_ref.shape[3]
    kc = tk // hw
    n_tiles = n_q * n_kv

    def stage_a(t, slot):
        qi, j = t // n_kv, t % n_kv
        tile_max = []
        for a in range(n_split):
            k_j = k_ref[0, pl.ds(pl.multiple_of(j * tk, tk), tk), :]
            s = _dot(k_j, qt_ref[0, qi * n_split + a])
            s_ref[slot, a] = s
            tile_max.append(jnp.max(s, axis=0, keepdims=True))
        return tuple(tile_max)

    def stage_b(t, slot, tile_max, ms):
        qi, j = t // n_kv, t % n_kv
        new_ms, alphas = [], []
        for a in range(n_split):
            m_old = jnp.where(j == 0, m0_ref[qi, a], ms[a])
            m_new = jnp.maximum(m_old, tile_max[a])
            alphas.append(jnp.exp2(m_old - m_new))
            p_ref[slot, a] = jnp.exp2(s_ref[slot, a] - m_new).astype(BF16)
            new_ms.append(m_new)
        return tuple(new_ms), tuple(alphas)

    def stage_c(t, slot, alphas):
        qi, j = t // n_kv, t % n_kv
        vt = jnp.concatenate([vt_ref[0, j * kc + c] for c in range(kc)], axis=1)
        for a in range(n_split):
            acc_ref[qi, a] = alphas[a] * acc_ref[qi, a] + _dot(vt, p_ref[slot, a])

    def step(t, slot, carry):
        ms, alphas, max_next = carry
        ms, new_alphas = stage_b(t + 1, (slot + 1) % ATTN_SLOTS, max_next, ms)
        max_after = stage_a(t + 2, (slot + 2) % ATTN_SLOTS)
        stage_c(t, slot, alphas)
        return ms, new_alphas, max_after

    groups = range(n_q * n_split)
    s_meta = [_dot(km_ref[0], qt_ref[0, g]) for g in groups]
    m_meta = [jnp.max(s, axis=0, keepdims=True) for s in s_meta]
    p_meta = [jnp.exp2(s - m).astype(BF16) for s, m in zip(s_meta, m_meta)]
    for g in groups:
        m0_ref[g // n_split, g % n_split] = m_meta[g]
        acc_ref[g // n_split, g % n_split] = _dot(vmt_ref[0], p_meta[g])

    max_next = stage_a(0, 0)
    ms, alphas = stage_b(0, 0, max_next, tuple(m_meta[:n_split]))
    carry = (ms, alphas, stage_a(1, 1))

    n_steps = n_tiles - 2
    n_loops = n_steps // unroll

    def body(i, carry):
        for u in range(unroll):
            carry = step(i * unroll + u, u % ATTN_SLOTS, carry)
        return carry

    carry = lax.fori_loop(0, n_loops, body, carry)
    for t in range(n_loops * unroll, n_steps):
        carry = step(t, t % ATTN_SLOTS, carry)
    ms, alphas, max_next = carry
    stage_c(n_tiles - 2, (n_tiles - 2) % ATTN_SLOTS, alphas)
    ms, alphas = stage_b(n_tiles - 1, (n_tiles - 1) % ATTN_SLOTS, max_next, ms)
    stage_c(n_tiles - 1, (n_tiles - 1) % ATTN_SLOTS, alphas)

    for g in groups:
        acc = acc_ref[g // n_split, g % n_split]
        o = jnp.transpose(acc[:MLA_D_V] * (1.0 / acc[MLA_D_V:MLA_D_V + 1]))
        rows = slice(g * hw, (g + 1) * hw)
        o_ref[rows, :] = (o * z_ref[rows, :].astype(F32)).astype(BF16)


def _attention(p, meta, batch, seq, tq, tk):
    hw = p['q'].shape[3]
    n_split, n_q, n_kv = tq // hw, seq // tq, seq // tk
    assert tq % hw == 0 and tk % hw == 0 and n_q * n_kv >= 3
    return pl.pallas_call(
        functools.partial(_attn_kernel, tk=tk, n_q=n_q, n_kv=n_kv, unroll=ATTN_UNROLL),
        grid=(batch, MLA_HEADS),
        in_specs=[pl.BlockSpec((1, seq // hw, MLA_QK_PAD, hw), lambda b, h: (h, b, 0, 0)),
                  pl.BlockSpec((1, seq, MLA_QK_PAD), lambda b, h: (h, b, 0)),
                  pl.BlockSpec((1, seq // hw, MLA_V_EXT, hw), lambda b, h: (h, b, 0, 0)),
                  pl.BlockSpec((1, N_META, MLA_QK_PAD), lambda b, h: (h, 0, 0)),
                  pl.BlockSpec((1, MLA_V_EXT, N_META), lambda b, h: (h, 0, 0)),
                  pl.BlockSpec((seq, MLA_D_V), lambda b, h: (b, h))],
        out_specs=pl.BlockSpec((seq, MLA_D_V), lambda b, h: (b, h)),
        out_shape=jax.ShapeDtypeStruct((batch * seq, MLA_WIDTH), BF16),
        scratch_shapes=[pltpu.VMEM((n_q, n_split, MLA_V_EXT, hw), F32),
                        pltpu.VMEM((n_q, n_split, 1, hw), F32),
                        pltpu.VMEM((ATTN_SLOTS, n_split, tk, hw), F32),
                        pltpu.VMEM((ATTN_SLOTS, n_split, tk, hw), BF16)],
        compiler_params=pltpu.CompilerParams(dimension_semantics=("arbitrary", "arbitrary"),
                                             vmem_limit_bytes=VMEM_LIMIT),
        name="mla_attention",
    )(p['q'], p['k'], p['v'], meta['k'], meta['vt'], p['mz'])


def _out_kernel(x_ref, of_ref, ob_ref, gr_ref, om_ref, sa_ref, sb_ref, lng_ref, lnb_ref, gng_ref, woa_ref,
                wob_ref, wout_ref, png_ref, pnb_ref, y_ref):
    h = _layer_norm(x_ref[...], lng_ref[...], lnb_ref[...])
    heads = []
    for hd in range(GLA_HEADS):
        cols = slice(hd * GLA_DV, (hd + 1) * GLA_DV)
        heads.append(_rms_norm(of_ref[:, cols].astype(F32) + ob_ref[:, cols].astype(F32), gng_ref[...]))
    a_in = (jnp.concatenate(heads, axis=1) * gr_ref[...].astype(F32)).astype(BF16)
    branch_a = _dot(a_in, woa_ref[...])
    branch_b = _dot(om_ref[...], wob_ref[...])
    mixed = sa_ref[...].astype(F32) * branch_a + sb_ref[...].astype(F32) * branch_b
    out = _dot(mixed.astype(BF16), wout_ref[...])
    y_ref[...] = _layer_norm(DEEPNORM_ALPHA * h + out, png_ref[...], pnb_ref[...])


def _out_stage(x2d, p, o_gla, o_mla, w, tm):
    n = x2d.shape[0]
    row = pl.BlockSpec((tm, D_MODEL), lambda i: (i, 0))
    consts = [w['emb_g'], w['emb_b'], w['gla_norm_g'], w['w_o_gla'], w['w_o_mla'], w['w_out'], w['post_g'],
              w['post_b']]
    o_fwd, o_bwd = o_gla
    return pl.pallas_call(
        _out_kernel,
        grid=(n // tm,),
        in_specs=[row] * 7 + [_const_spec(c.shape) for c in consts],
        out_specs=row,
        out_shape=jax.ShapeDtypeStruct((n, D_MODEL), F32),
        compiler_params=pltpu.CompilerParams(dimension_semantics=("arbitrary",), vmem_limit_bytes=VMEM_LIMIT),
        name="merge_out",
    )(x2d, o_fwd, o_bwd, p['gr'], o_mla, p['sa'], p['sb'], *consts)


def _rope_tables(start, length):
    inv_freq = 1.0 / (ROPE_THETA ** (jnp.arange(0, MLA_D_ROPE, 2, dtype=F32) / MLA_D_ROPE))
    ang = jnp.arange(start, start + length, dtype=F32)[:, None] * inv_freq[None, :]
    cos, sin = jnp.cos(ang), jnp.sin(ang)
    zero = jnp.zeros_like(cos)
    return (jnp.concatenate([cos, cos, zero, zero], axis=1),
            jnp.concatenate([zero, sin, zero, zero], axis=1),
            jnp.concatenate([-sin, zero, zero, zero], axis=1),
            jnp.concatenate([cos, cos, cos, cos], axis=1),
            jnp.concatenate([zero, sin, zero, sin], axis=1),
            jnp.concatenate([-sin, zero, -sin, zero], axis=1))


def _prepare_weights(emb_ln_g, emb_ln_b, w_in, b_merge, w_gla_gate_f, b_gla_gate_f, w_gla_gate_b, b_gla_gate_b,
                     gla_norm_g, w_o_gla, q_a_norm_g, w_q_b, kv_a_norm_g, w_kv_b, w_o_mla, w_out,
                     post_ln_g, post_ln_b):
    offs = [0]
    for s in IN_SPLITS:
        offs.append(offs[-1] + s)
    col = lambda i: w_in[0][:, offs[i]:offs[i + 1]]
    zcols = lambda n: jnp.zeros((D_MODEL, n), F32)
    w_sm = jnp.concatenate([col(8), col(4), col(5), zcols(LANES - MLA_D_ROPE - 2 * GLA_GATE_RANK),
                            col(6), col(7)], axis=1)
    gate_lo = MLA_D_ROPE
    w_gate = jnp.zeros((LANES, 2 * GLA_KEY_WIDTH), F32)
    w_gate = w_gate.at[gate_lo:gate_lo + GLA_GATE_RANK, :GLA_KEY_WIDTH].set(w_gla_gate_f[0])
    w_gate = w_gate.at[gate_lo + GLA_GATE_RANK:gate_lo + 2 * GLA_GATE_RANK, GLA_KEY_WIDTH:].set(w_gla_gate_b[0])
    wq = w_q_b[0].reshape(MLA_Q_RANK, MLA_HEADS, MLA_D_NOPE + MLA_D_ROPE)
    wq = jnp.concatenate([wq[:, :, :MLA_D_NOPE].reshape(MLA_Q_RANK, -1),
                          wq[:, :, MLA_D_NOPE:].reshape(MLA_Q_RANK, -1)], axis=1)
    wkv = w_kv_b[0].reshape(MLA_KV_RANK, MLA_HEADS, MLA_D_NOPE + MLA_D_V)
    wkv = jnp.concatenate([wkv[:, :, :MLA_D_NOPE].reshape(MLA_KV_RANK, -1),
                           wkv[:, :, MLA_D_NOPE:].reshape(MLA_KV_RANK, -1)], axis=1)
    r2 = lambda a: a.reshape(1, -1).astype(F32)

    def col_tiles(a):
        k, n = a.shape
        return a.astype(BF16).reshape(k, n // MXU_COLS, MXU_COLS).transpose(1, 0, 2)

    return {
        'emb_g': r2(emb_ln_g), 'emb_b': r2(emb_ln_b),
        'w_qk': col_tiles(jnp.concatenate([col(0), col(1)], axis=1)),
        'w_v': col_tiles(col(2)), 'w_r': col_tiles(col(3)), 'w_z': col_tiles(col(9)),
        'w_a': col_tiles(col(10)), 'w_b': col_tiles(col(11)), 'w_sm': w_sm.astype(BF16),
        'b_ma': r2(b_merge[0][:D_MODEL]), 'b_mb': r2(b_merge[0][D_MODEL:]),
        'w_gate': col_tiles(w_gate),
        'b_gate': r2(jnp.concatenate([b_gla_gate_f[0], b_gla_gate_b[0]])),
        'qn_g': r2(q_a_norm_g[0]), 'w_q': col_tiles(wq),
        'kvn_g': r2(kv_a_norm_g[0]), 'w_kv': col_tiles(wkv),
        'gla_norm_g': r2(gla_norm_g[0]),
        'w_o_gla': w_o_gla[0].astype(BF16), 'w_o_mla': w_o_mla[0].astype(BF16), 'w_out': w_out[0].astype(BF16),
        'post_g': r2(post_ln_g[0]), 'post_b': r2(post_ln_b[0]),
    }


def _pick_tile(n, target):
    t = min(n, target)
    while n % t:
        t //= 2
    return t


def _encode(x, w, meta, s0):
    batch, seq, _ = x.shape
    x2d = x.reshape(batch * seq, D_MODEL)
    tm = _pick_tile(seq, INPROJ_ROWS)
    p = _inproj(x2d, tm, seq // tm, w, _rope_tables(N_META, seq), True)
    o_gla = _gla(p, s0, batch, seq)
    o_mla = _attention(p, meta, batch, seq, _pick_tile(seq, ATTN_Q_ROWS), _pick_tile(seq, ATTN_K_ROWS))
    y = _out_stage(x2d, p, o_gla, o_mla, w, _pick_tile(seq, MERGE_ROWS))
    return y.reshape(batch, seq, D_MODEL)


def kernel(x_prompt, x_sample, meta_tokens, emb_ln_g, emb_ln_b, w_in, b_merge, w_gla_gate_f, b_gla_gate_f, w_gla_gate_b, b_gla_gate_b, gla_norm_g, w_o_gla, q_a_norm_g, w_q_b, kv_a_norm_g, w_kv_b, w_o_mla, w_out, post_ln_g, post_ln_b):
    w = _prepare_weights(emb_ln_g, emb_ln_b, w_in, b_merge, w_gla_gate_f, b_gla_gate_f, w_gla_gate_b,
                         b_gla_gate_b, gla_norm_g, w_o_gla, q_a_norm_g, w_q_b, kv_a_norm_g, w_kv_b, w_o_mla,
                         w_out, post_ln_g, post_ln_b)
    meta = _inproj(meta_tokens.astype(F32), N_META, 1, w, _rope_tables(0, N_META), False)
    vt = meta['v'].reshape(N_META, MLA_HEADS, MLA_D_V).transpose(1, 2, 0)
    ones_row = jnp.zeros((MLA_HEADS, MLA_V_EXT - MLA_D_V, N_META), BF16).at[:, 0, :].set(1.0)
    meta['vt'] = jnp.concatenate([vt, ones_row], axis=1)
    lead = ((GLA_CHUNK - N_META, 0), (0, 0))
    s0 = _gla_meta_state(jnp.pad(meta['gk'], lead), jnp.pad(meta['gv'], lead), jnp.pad(meta['gf'], lead))
    y_prompt = _encode(x_prompt, w, meta, s0)
    y_sample = _encode(x_sample, w, meta, s0)
    return (y_prompt, y_sample)
```
